```python
import jax
import jax.numpy as jnp
from jax import lax
import numpy as np

D_MODEL = 4096
BATCH = 4
SEQ = 2048
DEPTH = 2

CHUNK = 64
D_MIX = D_MODEL
D_POOL = D_MIX // 2
D_RWKV = D_MIX - D_POOL
POOL_WINDOWS = (2, 4, 8, 16)
N_POOL_GROUPS = len(POOL_WINDOWS)
D_POOL_GROUP = D_POOL // N_POOL_GROUPS
HEAD_SIZE = 64
N_RWKV_HEADS = D_RWKV // HEAD_SIZE
D_DECAY_LORA = max(32, int(round(1.8 * D_RWKV ** 0.5 / 32)) * 32)
D_ICLR_LORA = max(32, int(round(1.8 * D_RWKV ** 0.5 / 32)) * 32)
D_VRES_LORA = max(32, int(round(1.3 * D_RWKV ** 0.5 / 32)) * 32)
D_GATE_LORA = max(32, int(round(0.6 * D_RWKV ** 0.8 / 32)) * 32)
D_RWKV_IN = 3 * D_RWKV + D_DECAY_LORA + D_ICLR_LORA + D_GATE_LORA
D_IN = D_POOL + D_RWKV_IN
D_FF = 2 * D_MODEL
N_MOD = 9
RMS_EPS = 1e-6
GN_EPS = 64e-5
L2_EPS = 1e-12

kernel_name = 'hybrid_pool_rwkv7_macaron_adaln'


def _rmsnorm(x, gain):
    xf = x.astype(jnp.float32)
    y = xf * lax.rsqrt(jnp.mean(xf * xf, axis=-1, keepdims=True) + RMS_EPS)
    return (y * gain.astype(jnp.float32)).astype(x.dtype)


def _modulate(h, shift, scale):
    return h * (1 + scale[:, None, :]) + shift[:, None, :]


def _swiglu(h, w_in, w_out):
    gate, up = jnp.split(h @ w_in, 2, axis=-1)
    return (jax.nn.silu(gate) * up) @ w_out


def _time_shift(z):
    return jnp.pad(z[:, :-1], ((0, 0), (1, 0), (0, 0)))


def _pool_mixer(p, w_pool, pool_scale):
    b, t, _ = p.shape
    pf = p.astype(jnp.float32).reshape(b, t, N_POOL_GROUPS, D_POOL_GROUP)
    csum = jnp.cumsum(pf, axis=1)
    pos = jnp.arange(t)
    means = []
    for gi, win in enumerate(POOL_WINDOWS):
        cs = csum[:, :, gi]
        lagged = jnp.pad(cs, ((0, 0), (win, 0), (0, 0)))[:, :t]
        count = jnp.minimum(pos + 1, win).astype(jnp.float32)
        means.append((cs - lagged) / count[None, :, None])
    pooled = (jnp.stack(means, axis=2) - pf).astype(p.dtype)
    out = jnp.einsum('btgc,gcd->btgd', pooled, w_pool)
    return out.reshape(b, t, D_POOL) * pool_scale


def _wkv7_scan(r, w, k, v, a, b):
    bsz, t, h, n = r.shape
    n_chunks = t // CHUNK

    def to_chunks(u):
        return u.reshape(bsz, n_chunks, CHUNK, h, n).transpose(1, 2, 0, 3, 4)

    def frame_step(s, inp):
        r_t, w_t, k_t, v_t, a_t, b_t = inp
        sa = jnp.einsum('bhvk,bhk->bhv', s, a_t)
        s = s * w_t[:, :, None, :] + sa[..., None] * b_t[:, :, None, :] + v_t[..., None] * k_t[:, :, None, :]
        return s, jnp.einsum('bhvk,bhk->bhv', s, r_t)

    def chunk_step(s, chunk_inp):
        return lax.scan(frame_step, s, chunk_inp)

    s0 = jnp.zeros((bsz, h, n, n), jnp.float32)
    _, y = lax.scan(chunk_step, s0, tuple(to_chunks(u) for u in (r, w, k, v, a, b)))
    return y.transpose(2, 0, 1, 3, 4).reshape(bsz, t, h, n)


def _rwkv7_mixer(z, v_first, mu, w0, w2, a0, a2, g2, k_k, k_a, r_k, lnx_gain, lnx_bias, v0, v2):
    bsz, t, _ = z.shape
    out_dtype = z.dtype
    z = z.astype(jnp.float32)
    z = z + (_time_shift(z) - z) * mu
    splits = [int(s) for s in np.cumsum([D_RWKV, D_RWKV, D_RWKV, D_DECAY_LORA, D_ICLR_LORA, D_GATE_LORA])]
    r, k, v, w_lo, a_lo, g_lo, vres_lo = jnp.split(z, splits, axis=-1)

    def heads(u):
        return u.reshape(bsz, t, N_RWKV_HEADS, HEAD_SIZE)

    w_log = -jax.nn.softplus(-(w0 + jnp.tanh(w_lo) @ w2)) - 0.5
    decay = jnp.exp(-jnp.exp(w_log))
    iclr = jax.nn.sigmoid(a0 + a_lo @ a2)
    gate = jax.nn.sigmoid(g_lo) @ g2
    if v_first is None:
        v_first = v
    else:
        v = v + (v_first - v) * jax.nn.sigmoid(v0 + vres_lo @ v2)
    kk = heads(k * k_k)
    kk = kk / jnp.maximum(jnp.linalg.norm(kk, axis=-1, keepdims=True), L2_EPS)
    k = k * (1 + (iclr - 1) * k_a)
    rh, kh, vh = heads(r), heads(k), heads(v)
    y = _wkv7_scan(rh, heads(decay), kh, vh, -kk, kk * heads(iclr))
    mean = jnp.mean(y, axis=-1, keepdims=True)
    var = jnp.mean(jnp.square(y - mean), axis=-1, keepdims=True)
    y = ((y - mean) * lax.rsqrt(var + GN_EPS)).reshape(bsz, t, D_RWKV) * lnx_gain + lnx_bias
    bonus = jnp.sum(rh * kh * r_k, axis=-1, keepdims=True) * vh
    y = y + bonus.reshape(bsz, t, D_RWKV)
    return (y * gate).astype(out_dtype), v_first


def setup_inputs(seed: int = 0) -> dict:
    key = jax.random.key(seed)
    keys = iter(jax.random.split(key, 40))

    def nrm(shape, scale):
        return jax.random.normal(next(keys), shape, jnp.float32) * scale

    def uni(shape, lo, hi):
        return jax.random.uniform(next(keys), shape, jnp.float32, lo, hi)

    L, D, F = DEPTH, D_MODEL, D_FF
    return {
        'x': nrm((BATCH, SEQ, D), 1.0),
        'c': nrm((BATCH, D), 1.0),
        'ada_w': nrm((L, D, N_MOD * D), 0.5 * D ** -0.5),
        'ada_b': nrm((L, N_MOD * D), 0.02),
        'norm_gain': 1.0 + nrm((L, 3, D), 0.1),
        'ffn1_w_in': nrm((L, D, 2 * F), D ** -0.5),
        'ffn1_w_out': nrm((L, F, D), F ** -0.5),
        'ffn2_w_in': nrm((L, D, 2 * F), D ** -0.5),
        'ffn2_w_out': nrm((L, F, D), F ** -0.5),
        'mix_w_in': nrm((L, D, D_IN), D ** -0.5),
        'mix_w_in_vres': nrm((L - 1, D, D_VRES_LORA), D ** -0.5),
        'shift_mu': uni((L, D_RWKV_IN), 0.0, 1.0),
        'shift_mu_vres': uni((L - 1, D_VRES_LORA), 0.0, 1.0),
        'pool_w': nrm((L, N_POOL_GROUPS, D_POOL_GROUP, D_POOL_GROUP), D_POOL_GROUP ** -0.5),
        'pool_scale': 1.0 + nrm((L, D_POOL), 0.1),
        'decay_w0': uni((L, D_RWKV), -7.0, -1.0),
        'decay_w2': nrm((L, D_DECAY_LORA, D_RWKV), D_DECAY_LORA ** -0.5),
        'iclr_a0': nrm((L, D_RWKV), 0.1),
        'iclr_a2': nrm((L, D_ICLR_LORA, D_RWKV), D_ICLR_LORA ** -0.5),
        'gate_g2': nrm((L, D_GATE_LORA, D_RWKV), D_GATE_LORA ** -0.5),
        'vres_v0': 1.0 + nrm((L - 1, D_RWKV), 0.1),
        'vres_v2': nrm((L - 1, D_VRES_LORA, D_RWKV), D_VRES_LORA ** -0.5),
        'k_k': 0.85 + nrm((L, D_RWKV), 0.02),
        'k_a': 1.0 + nrm((L, D_RWKV), 0.02),
        'r_k': nrm((L, N_RWKV_HEADS, HEAD_SIZE), 0.1),
        'lnx_gain': 1.0 + nrm((L, D_RWKV), 0.1),
        'lnx_bias': nrm((L, D_RWKV), 0.01),
        'mix_w_out': nrm((L, D, D), D ** -0.5),
        'final_gain': 1.0 + nrm((D,), 0.1),
    }


def reference(x, c, ada_w, ada_b, norm_gain, ffn1_w_in, ffn1_w_out, ffn2_w_in, ffn2_w_out,
              mix_w_in, mix_w_in_vres, shift_mu, shift_mu_vres, pool_w, pool_scale,
              decay_w0, decay_w2, iclr_a0, iclr_a2, gate_g2, vres_v0, vres_v2,
              k_k, k_a, r_k, lnx_gain, lnx_bias, mix_w_out, final_gain):
    b = x.shape[0]
    c_act = jax.nn.silu(c)
    v_first = None
    for l in range(DEPTH):
        mod = (c_act @ ada_w[l] + ada_b[l]).reshape(b, N_MOD, D_MODEL)
        h = _modulate(_rmsnorm(x, norm_gain[l, 0]), mod[:, 0], mod[:, 1])
        x = x + 0.5 * mod[:, 2][:, None, :] * _swiglu(h, ffn1_w_in[l], ffn1_w_out[l])
        h = _modulate(_rmsnorm(x, norm_gain[l, 1]), mod[:, 3], mod[:, 4])
        if l == 0:
            w_in, mu, v0, v2 = mix_w_in[l], shift_mu[l], None, None
        else:
            w_in = jnp.concatenate([mix_w_in[l], mix_w_in_vres[l - 1]], axis=1)
            mu = jnp.concatenate([shift_mu[l], shift_mu_vres[l - 1]])
            v0, v2 = vres_v0[l - 1], vres_v2[l - 1]
        proj = h @ w_in
        pool_out = _pool_mixer(proj[..., :D_POOL], pool_w[l], pool_scale[l])
        rwkv_out, v_first = _rwkv7_mixer(proj[..., D_POOL:], v_first, mu,
                                         decay_w0[l], decay_w2[l], iclr_a0[l], iclr_a2[l], gate_g2[l],
                                         k_k[l], k_a[l], r_k[l], lnx_gain[l], lnx_bias[l], v0, v2)
        mixed = jnp.concatenate([pool_out, rwkv_out], axis=-1) @ mix_w_out[l]
        x = x + mod[:, 5][:, None, :] * mixed
        h = _modulate(_rmsnorm(x, norm_gain[l, 2]), mod[:, 6], mod[:, 7])
        x = x + 0.5 * mod[:, 8][:, None, :] * _swiglu(h, ffn2_w_in[l], ffn2_w_out[l])
    return _rmsnorm(x, final_gain)
```

```python
import functools

import jax
import jax.numpy as jnp
from jax import lax
from jax.experimental import pallas as pl
from jax.experimental.pallas import tpu as pltpu

F32 = jnp.float32
BF16 = jnp.bfloat16

LANES = 128
SUBLANES = 8
HEAD_SIZE = 64
PAIR = 2 * HEAD_SIZE
CHUNK = 64
POOL_WINDOWS = (2, 4, 8, 16)
N_MOD = 9
RMS_EPS = 1e-6
GN_EPS = 64e-5
L2_EPS = 1e-12
VMEM_LIMIT_BYTES = 56 * 1024 * 1024


def _params(n_axes):
    return pltpu.CompilerParams(dimension_semantics=("arbitrary",) * n_axes,
                                vmem_limit_bytes=VMEM_LIMIT_BYTES)


def _dot(a, b):
    return jnp.dot(a, b, preferred_element_type=F32)


def _dot_nt(a, b):
    return lax.dot_general(a, b, (((1,), (1,)), ((), ())), preferred_element_type=F32)


def _dot_tn(a, b):
    return lax.dot_general(a, b, (((0,), (0,)), ((), ())), preferred_element_type=F32)


def _split2(x):
    hi = x.astype(BF16)
    lo = (x - hi.astype(F32)).astype(BF16)
    return hi, lo


def _split3(x):
    hi = x.astype(BF16)
    r1 = x - hi.astype(F32)
    mid = r1.astype(BF16)
    lo = (r1 - mid.astype(F32)).astype(BF16)
    return hi, mid, lo


def _head_ones(n):
    r = lax.broadcasted_iota(jnp.int32, (n, n), 0) // HEAD_SIZE
    c = lax.broadcasted_iota(jnp.int32, (n, n), 1) // HEAD_SIZE
    return (r == c).astype(BF16)


def _head_sum(x, ones):
    hi, lo = _split2(x)
    cols = []
    for j in range(x.shape[1] // LANES):
        sl = slice(j * LANES, (j + 1) * LANES)
        cols.append(_dot(hi[:, sl], ones) + _dot(lo[:, sl], ones))
    return cols[0] if len(cols) == 1 else jnp.concatenate(cols, axis=1)


def _ada_kernel(c_ref, w_ref, b_ref, o_ref):
    c = c_ref[...]
    act = (c * jax.nn.sigmoid(c)).astype(BF16)
    o_ref[...] = _dot(act, w_ref[...].astype(BF16)) + b_ref[...]


def _ada(c_pad, ada_w, ada_b, tn):
    depth, d, n = ada_w.shape
    rows = c_pad.shape[0]
    return pl.pallas_call(
        _ada_kernel,
        grid=(depth, n // tn),
        in_specs=[pl.BlockSpec((rows, d), lambda l, j: (0, 0)),
                  pl.BlockSpec((None, d, tn), lambda l, j: (l, 0, j)),
                  pl.BlockSpec((None, 1, tn), lambda l, j: (l, 0, j))],
        out_specs=pl.BlockSpec((None, rows, tn), lambda l, j: (l, 0, j)),
        out_shape=jax.ShapeDtypeStruct((depth, rows, n), F32),
        compiler_params=_params(2),
        name="ada_mod",
    )(c_pad, ada_w, ada_b.reshape(depth, 1, n))


def _norm_mod_kernel(x_ref, g_ref, sh_ref, sc_ref, o_ref):
    x = x_ref[...]
    y = x * lax.rsqrt(jnp.mean(x * x, axis=-1, keepdims=True) + RMS_EPS) * g_ref[...]
    o_ref[...] = (y * (1 + sc_ref[...]) + sh_ref[...]).astype(o_ref.dtype)


def _norm_mod(x, gain, mod, row_of, j_shift, j_scale, seq, tm):
    m, d = x.shape
    per_seq = seq // tm
    return pl.pallas_call(
        _norm_mod_kernel,
        grid=(m // tm,),
        in_specs=[pl.BlockSpec((tm, d), lambda i: (i, 0)),
                  pl.BlockSpec((1, d), lambda i: (0, 0)),
                  pl.BlockSpec((None, 1, d), lambda i: (row_of(i // per_seq, j_shift), 0, 0)),
                  pl.BlockSpec((None, 1, d), lambda i: (row_of(i // per_seq, j_scale), 0, 0))],
        out_specs=pl.BlockSpec((tm, d), lambda i: (i, 0)),
        out_shape=jax.ShapeDtypeStruct((m, d), BF16),
        compiler_params=_params(1),
        name="norm_mod",
    )(x, gain.reshape(1, d), mod, mod)


def _final_norm_kernel(x_ref, g_ref, o_ref):
    x = x_ref[...]
    o_ref[...] = x * lax.rsqrt(jnp.mean(x * x, axis=-1, keepdims=True) + RMS_EPS) * g_ref[...]


def _final_norm(x, gain, tm):
    m, d = x.shape
    return pl.pallas_call(
        _final_norm_kernel,
        grid=(m // tm,),
        in_specs=[pl.BlockSpec((tm, d), lambda i: (i, 0)),
                  pl.BlockSpec((1, d), lambda i: (0, 0))],
        out_specs=pl.BlockSpec((tm, d), lambda i: (i, 0)),
        out_shape=jax.ShapeDtypeStruct((m, d), F32),
        compiler_params=_params(1),
        name="final_norm",
    )(x, gain.reshape(1, d))


def _mm_plain_kernel(a_ref, w_ref, o_ref, wb_ref):
    @pl.when(pl.program_id(1) == 0)
    def _():
        wb_ref[...] = w_ref[...].astype(BF16)
    o_ref[...] = _dot(a_ref[...], wb_ref[...]).astype(o_ref.dtype)


def _mm_plain(a, w, tm, tn, out_dtype):
    m, k = a.shape
    n = w.shape[1]
    return pl.pallas_call(
        _mm_plain_kernel,
        grid=(pl.cdiv(n, tn), m // tm),
        in_specs=[pl.BlockSpec((tm, k), lambda j, i: (i, 0)),
                  pl.BlockSpec((k, tn), lambda j, i: (0, j))],
        out_specs=pl.BlockSpec((tm, tn), lambda j, i: (i, j)),
        out_shape=jax.ShapeDtypeStruct((m, n), out_dtype),
        scratch_shapes=[pltpu.VMEM((k, tn), BF16)],
        compiler_params=_params(2),
        name="mm_plain",
    )(a, w)


def _mm_swiglu_kernel(a_ref, wg_ref, wu_ref, o_ref, wgb_ref, wub_ref):
    @pl.when(pl.program_id(1) == 0)
    def _():
        wgb_ref[...] = wg_ref[...].astype(BF16)
        wub_ref[...] = wu_ref[...].astype(BF16)
    a = a_ref[...]
    g = _dot(a, wgb_ref[...])
    u = _dot(a, wub_ref[...])
    o_ref[...] = (g * jax.nn.sigmoid(g) * u).astype(o_ref.dtype)


def _mm_swiglu(a, w_in, tm, tn):
    m, k = a.shape
    f = w_in.shape[1] // 2
    nb = f // tn
    return pl.pallas_call(
        _mm_swiglu_kernel,
        grid=(nb, m // tm),
        in_specs=[pl.BlockSpec((tm, k), lambda j, i: (i, 0)),
                  pl.BlockSpec((k, tn), lambda j, i: (0, j)),
                  pl.BlockSpec((k, tn), lambda j, i: (0, j + nb))],
        out_specs=pl.BlockSpec((tm, tn), lambda j, i: (i, j)),
        out_shape=jax.ShapeDtypeStruct((m, f), BF16),
        scratch_shapes=[pltpu.VMEM((k, tn), BF16), pltpu.VMEM((k, tn), BF16)],
        compiler_params=_params(2),
        name="mm_swiglu",
    )(a, w_in, w_in)


def _mm_resid_kernel(a_ref, w_ref, x_ref, g_ref, o_ref, wb_ref, *, coef):
    @pl.when(pl.program_id(1) == 0)
    def _():
        wb_ref[...] = w_ref[...].astype(BF16)
    o_ref[...] = x_ref[...] + (coef * g_ref[...]) * _dot(a_ref[...], wb_ref[...])


def _mm_resid(a, w, x, mod, row_of, j_gate, coef, seq, tm, tn):
    m, k = a.shape
    n = w.shape[1]
    per_seq = seq // tm
    return pl.pallas_call(
        functools.partial(_mm_resid_kernel, coef=coef),
        grid=(n // tn, m // tm),
        in_specs=[pl.BlockSpec((tm, k), lambda j, i: (i, 0)),
                  pl.BlockSpec((k, tn), lambda j, i: (0, j)),
                  pl.BlockSpec((tm, tn), lambda j, i: (i, j)),
                  pl.BlockSpec((None, 1, tn), lambda j, i: (row_of(i // per_seq, j_gate), 0, j))],
        out_specs=pl.BlockSpec((tm, tn), lambda j, i: (i, j)),
        out_shape=jax.ShapeDtypeStruct((m, n), F32),
        scratch_shapes=[pltpu.VMEM((k, tn), BF16)],
        compiler_params=_params(2),
        name="mm_resid",
    )(a, w, x, mod)


def _shift_rows(x, k, row):
    return jnp.where(row < k, 0.0, pltpu.roll(x, k, axis=0))


def _pool_kernel(p_ref, w_ref, s_ref, o_ref, *, win):
    x = p_ref[...]
    row = lax.broadcasted_iota(jnp.int32, x.shape, 0)
    acc = x
    span = 1
    while span < win:
        acc = acc + _shift_rows(acc, span, row)
        span *= 2
    count = jnp.minimum(row + 1, win).astype(F32)
    pooled = (acc / count - x).astype(BF16)
    o_ref[...] = (_dot(pooled, w_ref[...].astype(BF16)) * s_ref[...]).astype(o_ref.dtype)


def _pool_group(proj, pool_w_l, pool_scale_l, gi, batch, seq):
    cg = pool_w_l.shape[1]
    m = proj.shape[0]
    return pl.pallas_call(
        functools.partial(_pool_kernel, win=POOL_WINDOWS[gi]),
        grid=(batch,),
        in_specs=[pl.BlockSpec((seq, cg), lambda b: (b, gi)),
                  pl.BlockSpec((None, cg, cg), lambda b: (gi, 0, 0)),
                  pl.BlockSpec((1, cg), lambda b: (0, gi))],
        out_specs=pl.BlockSpec((seq, cg), lambda b: (b, 0)),
        out_shape=jax.ShapeDtypeStruct((m, cg), BF16),
        compiler_params=_params(1),
        name=f"pool_g{gi}",
    )(proj, pool_w_l, pool_scale_l)


def _prep_kernel(*refs, has_vres, lora_bounds, n_pairs, tiles_per_seq):
    if has_vres:
        (r_ref, k_ref, v_ref, lo_ref, rp_ref, kp_ref, vp_ref, lop_ref,
         mur_ref, muk_ref, muv_ref, mulo_ref, w0_ref, a0_ref, kk_ref, ka_ref, rk_ref,
         w2_ref, a2_ref, g2_ref, v0_ref, v2_ref, vf_ref,
         rt_ref, at_ref, kt_ref, bt_ref, kh_ref, bh_ref, vv_ref, gate_ref, bonus_ref, pc_ref) = refs
    else:
        (r_ref, k_ref, v_ref, lo_ref, rp_ref, kp_ref, vp_ref, lop_ref,
         mur_ref, muk_ref, muv_ref, mulo_ref, w0_ref, a0_ref, kk_ref, ka_ref, rk_ref,
         w2_ref, a2_ref, g2_ref,
         rt_ref, at_ref, kt_ref, bt_ref, kh_ref, bh_ref, vv_ref, gate_ref, bonus_ref, pc_ref) = refs
    first = (pl.program_id(0) % tiles_per_seq) == 0
    tm = r_ref.shape[0]

    def shift_mix(cur_ref, prev_ref, mu_ref):
        z = cur_ref[...]
        row = lax.broadcasted_iota(jnp.int32, z.shape, 0)
        prev_last = jnp.where(first, 0.0, prev_ref[SUBLANES - 1:SUBLANES, :])
        zp = jnp.where(row == 0, prev_last, pltpu.roll(z, 1, axis=0))
        return z + (zp - z) * mu_ref[...]

    r = shift_mix(r_ref, rp_ref, mur_ref)
    k = shift_mix(k_ref, kp_ref, muk_ref)
    v = shift_mix(v_ref, vp_ref, muv_ref)
    lo = shift_mix(lo_ref, lop_ref, mulo_ref)

    e_w, e_a, e_g = lora_bounds
    lane = lax.broadcasted_iota(jnp.int32, lo.shape, 1)
    act = jnp.where(lane < e_w, jnp.tanh(lo),
                    jnp.where((lane >= e_a) & (lane < e_g), jax.nn.sigmoid(lo), lo)).astype(BF16)
    dlog = -(w0_ref[...] + _dot(act, w2_ref[...].astype(BF16)))
    softplus = jnp.maximum(dlog, 0.0) + jnp.log1p(jnp.exp(-jnp.abs(dlog)))
    lw = -jnp.exp(-softplus - 0.5)
    iclr = jax.nn.sigmoid(a0_ref[...] + _dot(act, a2_ref[...].astype(BF16)))
    gate = _dot(act, g2_ref[...].astype(BF16))
    if has_vres:
        vf = jnp.concatenate([vf_ref[p] for p in range(n_pairs)], axis=1)
        v = v + (vf - v) * jax.nn.sigmoid(v0_ref[...] + _dot(act, v2_ref[...].astype(BF16)))

    ones = _head_ones(LANES)
    kk = k * kk_ref[...]
    kk = kk / jnp.maximum(jnp.sqrt(_head_sum(kk * kk, ones)), L2_EPS)
    km = k * (1 + (iclr - 1) * ka_ref[...])
    bonus = _head_sum(r * km * rk_ref[...], ones) * v
    a = -kk
    b = kk * iclr

    ri = lax.broadcasted_iota(jnp.int32, (tm, tm), 0)
    ci = lax.broadcasted_iota(jnp.int32, (tm, tm), 1)
    same = (ri // CHUNK) == (ci // CHUNK)
    sel = jnp.concatenate([(same & (ci <= ri)).astype(BF16),
                           (same & (ci > ri)).astype(BF16),
                           same.astype(BF16)], axis=0)
    h3, m3, l3 = _split3(lw)
    sums = _dot(sel, h3) + _dot(sel, m3) + _dot(sel, l3)
    cum, rest, tot = sums[:tm], sums[tm:2 * tm], sums[2 * tm:]
    p_in = jnp.exp(cum)
    p_ex = jnp.exp(cum - lw)
    p_inv = jnp.exp(-cum)
    p_rest = jnp.exp(rest)
    p_tot = jnp.exp(tot)

    outs = ((rt_ref, r * p_in), (at_ref, a * p_ex), (kt_ref, km * p_inv), (bt_ref, b * p_inv),
            (kh_ref, km * p_rest), (bh_ref, b * p_rest), (vv_ref, v), (gate_ref, gate),
            (bonus_ref, bonus))
    for ref, val in outs:
        for p in range(n_pairs):
            ref[p] = val[:, p * PAIR:(p + 1) * PAIR]
    for p in range(n_pairs):
        for c in range(tm // CHUNK):
            pc_ref[p, c] = p_tot[c * CHUNK:c * CHUNK + SUBLANES, p * PAIR:(p + 1) * PAIR]


def _prep(proj, vfirst, rows, mats, lora_bounds, batch, seq, d_rwkv, lo_blk, tm):
    m = proj.shape[0]
    has_vres = vfirst is not None
    n_pairs = d_rwkv // PAIR
    tiles_per_seq = seq // tm
    pool_blocks = 1
    lo_col = (4 * d_rwkv) // lo_blk
    sub = tm // SUBLANES

    def cur(width, col):
        return pl.BlockSpec((tm, width), lambda i: (i, col))

    def prev(width, col):
        return pl.BlockSpec((SUBLANES, width), lambda i: (jnp.maximum(i * sub - 1, 0), col))

    def row(width):
        return pl.BlockSpec((1, width), lambda i: (0, 0))

    in_specs = [cur(d_rwkv, pool_blocks), cur(d_rwkv, pool_blocks + 1), cur(d_rwkv, pool_blocks + 2),
                cur(lo_blk, lo_col),
                prev(d_rwkv, pool_blocks), prev(d_rwkv, pool_blocks + 1), prev(d_rwkv, pool_blocks + 2),
                prev(lo_blk, lo_col),
                row(d_rwkv), row(d_rwkv), row(d_rwkv), row(lo_blk)]
    args = [proj] * 8 + [rows["mu_r"], rows["mu_k"], rows["mu_v"], rows["mu_lo"]]
    for name in ("w0", "a0", "k_k", "k_a", "r_k"):
        in_specs.append(row(d_rwkv))
        args.append(rows[name])
    for name in ("w2", "a2", "g2"):
        in_specs.append(pl.BlockSpec((lo_blk, d_rwkv), lambda i: (0, 0)))
        args.append(mats[name])
    if has_vres:
        in_specs += [row(d_rwkv), pl.BlockSpec((lo_blk, d_rwkv), lambda i: (0, 0)),
                     pl.BlockSpec((None, n_pairs, tm, PAIR),
                                  lambda i: (i // tiles_per_seq, 0, i % tiles_per_seq, 0))]
        args += [rows["v0"], mats["v2"], vfirst]

    pair_spec = pl.BlockSpec((None, n_pairs, tm, PAIR),
                             lambda i: (i // tiles_per_seq, 0, i % tiles_per_seq, 0))
    pair_shape = jax.ShapeDtypeStruct((batch, n_pairs, seq, PAIR), F32)
    cpt = tm // CHUNK
    pc_spec = pl.BlockSpec((None, n_pairs, cpt, SUBLANES, PAIR),
                           lambda i: (i // tiles_per_seq, 0, i % tiles_per_seq, 0, 0))
    pc_shape = jax.ShapeDtypeStruct((batch, n_pairs, seq // CHUNK, SUBLANES, PAIR), F32)
    return pl.pallas_call(
        functools.partial(_prep_kernel, has_vres=has_vres, lora_bounds=lora_bounds,
                          n_pairs=n_pairs, tiles_per_seq=tiles_per_seq),
        grid=(m // tm,),
        in_specs=in_specs,
        out_specs=[pair_spec] * 9 + [pc_spec],
        out_shape=[pair_shape] * 9 + [pc_shape],
        compiler_params=_params(1),
        name="rwkv_prep",
    )(*args)


def _mm1(a, b, fn):
    return fn(a.astype(BF16), b.astype(BF16))


def _mm3(a, b, fn):
    ah, al = _split2(a)
    bh, bl = _split2(b)
    return fn(ah, bh) + fn(ah, bl) + fn(al, bh)


def _wkv_kernel(rt_ref, at_ref, kt_ref, bt_ref, kh_ref, bh_ref, v_ref, gate_ref, bonus_ref, pc_ref,
                lg_ref, lb_ref, o_ref, s_ref):
    @pl.when(pl.program_id(2) == 0)
    def _():
        s_ref[...] = jnp.zeros_like(s_ref)

    n_chunks = rt_ref.shape[0] // CHUNK
    lane = lax.broadcasted_iota(jnp.int32, (CHUNK, PAIR), 1)
    head0 = lane < HEAD_SIZE
    ri = lax.broadcasted_iota(jnp.int32, (PAIR, PAIR), 0)
    ci = lax.broadcasted_iota(jnp.int32, (PAIR, PAIR), 1)
    strict = ri > ci
    incl = ri >= ci
    eye = (ri == ci).astype(F32)
    ones = _head_ones(PAIR)
    lnx_g = lg_ref[...]
    lnx_b = lb_ref[...]

    def stack(x):
        return jnp.concatenate([jnp.where(head0, x, 0.0), jnp.where(head0, 0.0, x)], axis=0)

    def chunk(c, carry):
        sl = pl.ds(pl.multiple_of(c * CHUNK, CHUNK), CHUNK)
        a_s, r_s = stack(at_ref[sl, :]), stack(rt_ref[sl, :])
        b_s, k_s = stack(bt_ref[sl, :]), stack(kt_ref[sl, :])
        bh_s, kh_s = stack(bh_ref[sl, :]), stack(kh_ref[sl, :])
        v_s = stack(v_ref[sl, :])
        s = s_ref[...]

        ar = jnp.concatenate([a_s, r_s], axis=0)
        bk = jnp.concatenate([b_s, k_s], axis=0)
        amat = _mm3(ar, bk, _dot_nt)
        a_ab = jnp.where(strict, amat[:PAIR, :PAIR], 0.0)
        a_ak = jnp.where(strict, amat[:PAIR, PAIR:], 0.0)
        a_rb = jnp.where(incl, amat[PAIR:, :PAIR], 0.0)
        a_rk = jnp.where(incl, amat[PAIR:, PAIR:], 0.0)

        pw = a_ab
        inv = eye + pw
        span = 1
        while 2 * span < CHUNK:
            pw = _mm1(pw, pw, _dot)
            inv = inv + _mm1(pw, inv, _dot)
            span *= 2

        xs = _mm1(ar, s, _dot_nt)
        u_s = _mm1(inv, xs[:PAIR] + _mm1(a_ak, v_s, _dot), _dot)
        uv = jnp.concatenate([u_s, v_s], axis=0)
        y_s = xs[PAIR:] + _mm1(jnp.concatenate([a_rb, a_rk], axis=1), uv, _dot)
        y = y_s[:CHUNK] + y_s[CHUNK:]

        s_ref[...] = s * pc_ref[c][0:1, :] + _mm1(uv, jnp.concatenate([bh_s, kh_s], axis=0), _dot_tn)

        mean = _head_sum(y, ones) * (1.0 / HEAD_SIZE)
        yc = y - mean
        var = _head_sum(yc * yc, ones) * (1.0 / HEAD_SIZE)
        yn = yc * lax.rsqrt(var + GN_EPS) * lnx_g + lnx_b + bonus_ref[sl, :]
        o_ref[sl, :] = (yn * gate_ref[sl, :]).astype(o_ref.dtype)
        return carry

    lax.fori_loop(0, n_chunks, chunk, 0)


def _wkv(prep_out, lnx_gain, lnx_bias, batch, seq, tc):
    rt, at, kt, bt, kh, bh, vv, gate, bonus, pc = prep_out
    n_pairs = rt.shape[1]
    n_tc = seq // tc
    blk = pl.BlockSpec((None, None, tc, PAIR), lambda b, p, c: (b, p, c, 0))
    pc_blk = pl.BlockSpec((None, None, tc // CHUNK, SUBLANES, PAIR), lambda b, p, c: (b, p, c, 0, 0))
    row = pl.BlockSpec((1, PAIR), lambda b, p, c: (0, p))
    return pl.pallas_call(
        _wkv_kernel,
        grid=(batch, n_pairs, n_tc),
        in_specs=[blk] * 9 + [pc_blk, row, row],
        out_specs=pl.BlockSpec((tc, PAIR), lambda b, p, c: (b * n_tc + c, p)),
        out_shape=jax.ShapeDtypeStruct((batch * seq, n_pairs * PAIR), BF16),
        scratch_shapes=[pltpu.VMEM((PAIR, PAIR), F32)],
        compiler_params=_params(3),
        name="wkv7",
    )(rt, at, kt, bt, kh, bh, vv, gate, bonus, pc, lnx_gain, lnx_bias)


def _tile(n, target):
    t = min(n, target)
    while n % t:
        t //= 2
    return t


def kernel(x, c, ada_w, ada_b, norm_gain, ffn1_w_in, ffn1_w_out, ffn2_w_in, ffn2_w_out, mix_w_in,
           mix_w_in_vres, shift_mu, shift_mu_vres, pool_w, pool_scale, decay_w0, decay_w2, iclr_a0,
           iclr_a2, gate_g2, vres_v0, vres_v2, k_k, k_a, r_k, lnx_gain, lnx_bias, mix_w_out,
           final_gain):
    batch, seq, d = x.shape
    depth = ada_w.shape[0]
    m = batch * seq
    d_pool = pool_scale.shape[1]
    d_rwkv = decay_w0.shape[1]
    n_groups = pool_w.shape[1]
    d_w, d_a, d_g = decay_w2.shape[1], iclr_a2.shape[1], gate_g2.shape[1]
    d_v = vres_v2.shape[1]
    lo_blk = d_w + d_a + d_g + d_v
    assert d_pool == d_rwkv and lo_blk % LANES == 0 and (4 * d_rwkv) % lo_blk == 0
    assert d_rwkv % PAIR == 0 and seq % (2 * CHUNK) == 0
    lora_bounds = (d_w, d_w + d_a, d_w + d_a + d_g)

    tm_big = _tile(seq, 1024)
    tm_mid = _tile(seq, 512)
    tm_norm = _tile(seq, 256)
    tm_prep = 2 * CHUNK
    tc = _tile(seq, 512)

    c_pad = jnp.pad(c, ((0, SUBLANES - batch), (0, 0)))
    mod = _ada(c_pad, ada_w, ada_b, _tile(N_MOD * d, 512))
    mod = mod.reshape(depth * SUBLANES * N_MOD, 1, d)

    xf = x.reshape(m, d)
    vfirst = None
    for l in range(depth):
        def row_of(b, j, l=l):
            return (l * SUBLANES + b) * N_MOD + j

        h = _norm_mod(xf, norm_gain[l, 0], mod, row_of, 0, 1, seq, tm_norm)
        act = _mm_swiglu(h, ffn1_w_in[l], tm_big, _tile(ffn1_w_in.shape[2] // 2, 256))
        xf = _mm_resid(act, ffn1_w_out[l], xf, mod, row_of, 2, 0.5, seq, tm_mid, _tile(d, 256))

        h = _norm_mod(xf, norm_gain[l, 1], mod, row_of, 3, 4, seq, tm_norm)
        if l == 0:
            w_tail = jnp.zeros((d, d_v), F32)
            mu_tail = jnp.zeros((d_v,), F32)
        else:
            w_tail = mix_w_in_vres[l - 1]
            mu_tail = shift_mu_vres[l - 1]
        w_in = jnp.concatenate([mix_w_in[l], w_tail], axis=1)
        mu = jnp.concatenate([shift_mu[l], mu_tail])
        proj = _mm_plain(h, w_in, tm_big, _tile(w_in.shape[1], 512), F32)

        pool_out = [_pool_group(proj, pool_w[l], pool_scale[l].reshape(1, d_pool), gi, batch, seq)
                    for gi in range(n_groups)]

        def pad_rows(w, start):
            return jnp.pad(w, ((start, lo_blk - start - w.shape[0]), (0, 0)))

        rows = {"mu_r": mu[0:d_rwkv], "mu_k": mu[d_rwkv:2 * d_rwkv], "mu_v": mu[2 * d_rwkv:3 * d_rwkv],
                "mu_lo": mu[3 * d_rwkv:], "w0": decay_w0[l], "a0": iclr_a0[l], "k_k": k_k[l],
                "k_a": k_a[l], "r_k": r_k[l].reshape(-1)}
        mats = {"w2": pad_rows(decay_w2[l], 0), "a2": pad_rows(iclr_a2[l], d_w),
                "g2": pad_rows(gate_g2[l], d_w + d_a)}
        if l > 0:
            rows["v0"] = vres_v0[l - 1]
            mats["v2"] = pad_rows(vres_v2[l - 1], d_w + d_a + d_g)
        rows = {k_: v_.reshape(1, -1) for k_, v_ in rows.items()}
        prep_out = _prep(proj, vfirst, rows, mats, lora_bounds, batch, seq, d_rwkv, lo_blk, tm_prep)
        if l == 0:
            vfirst = prep_out[6]
        rwkv_out = _wkv(prep_out, lnx_gain[l].reshape(1, -1), lnx_bias[l].reshape(1, -1), batch, seq, tc)

        mixed_in = jnp.concatenate(pool_out + [rwkv_out], axis=1)
        xf = _mm_resid(mixed_in, mix_w_out[l], xf, mod, row_of, 5, 1.0, seq, tm_big, _tile(d, 512))

        h = _norm_mod(xf, norm_gain[l, 2], mod, row_of, 6, 7, seq, tm_norm)
        act = _mm_swiglu(h, ffn2_w_in[l], tm_big, _tile(ffn2_w_in.shape[2] // 2, 256))
        xf = _mm_resid(act, ffn2_w_out[l], xf, mod, row_of, 8, 0.5, seq, tm_mid, _tile(d, 256))

    return _final_norm(xf, final_gain, tm_norm).reshape(batch, seq, d)
```

```python
import functools

import jax
import jax.numpy as jnp
from jax import lax
from jax.experimental import pallas as pl
from jax.experimental.pallas import tpu as pltpu

F32 = jnp.float32
BF16 = jnp.bfloat16

LANES = 128
SUBLANES = 8
HEAD_SIZE = 64
PAIR = 2 * HEAD_SIZE
CHUNK = 64
POOL_WINDOWS = (2, 4, 8, 16)
N_MOD = 9
RMS_EPS = 1e-6
GN_EPS = 64e-5
L2_EPS = 1e-12
VMEM_LIMIT_BYTES = 56 * 1024 * 1024


def _params(n_axes):
    return pltpu.CompilerParams(dimension_semantics=("arbitrary",) * n_axes,
                                vmem_limit_bytes=VMEM_LIMIT_BYTES)


def _dot(a, b):
    return jnp.dot(a, b, preferred_element_type=F32)


def _dot_nt(a, b):
    return lax.dot_general(a, b, (((1,), (1,)), ((), ())), preferred_element_type=F32)


def _dot_tn(a, b):
    return lax.dot_general(a, b, (((0,), (0,)), ((), ())), preferred_element_type=F32)


def _split2(x):
    hi = x.astype(BF16)
    lo = (x - hi.astype(F32)).astype(BF16)
    return hi, lo


def _split3(x):
    hi = x.astype(BF16)
    r1 = x - hi.astype(F32)
    mid = r1.astype(BF16)
    lo = (r1 - mid.astype(F32)).astype(BF16)
    return hi, mid, lo


def _head_ones(n):
    r = lax.broadcasted_iota(jnp.int32, (n, n), 0) // HEAD_SIZE
    c = lax.broadcasted_iota(jnp.int32, (n, n), 1) // HEAD_SIZE
    return (r == c).astype(BF16)


def _head_sum(x, ones):
    hi, lo = _split2(x)
    cols = []
    for j in range(x.shape[1] // LANES):
        sl = slice(j * LANES, (j + 1) * LANES)
        cols.append(_dot(hi[:, sl], ones) + _dot(lo[:, sl], ones))
    return cols[0] if len(cols) == 1 else jnp.concatenate(cols, axis=1)


def _ada_kernel(c_ref, w_ref, b_ref, o_ref):
    k = pl.program_id(2)
    c = c_ref[...]
    act = (c * jax.nn.sigmoid(c)).astype(BF16)
    part = _dot(act, w_ref[...].astype(BF16))

    @pl.when(k == 0)
    def _():
        o_ref[...] = part + b_ref[...]

    @pl.when(k > 0)
    def _():
        o_ref[...] += part


def _ada(c_pad, ada_w, ada_b, tk, tn):
    depth, d, n = ada_w.shape
    rows = c_pad.shape[0]
    return pl.pallas_call(
        _ada_kernel,
        grid=(depth, n // tn, d // tk),
        in_specs=[pl.BlockSpec((rows, tk), lambda l, j, k: (0, k)),
                  pl.BlockSpec((None, tk, tn), lambda l, j, k: (l, k, j)),
                  pl.BlockSpec((None, 1, tn), lambda l, j, k: (l, 0, j))],
        out_specs=pl.BlockSpec((None, rows, tn), lambda l, j, k: (l, 0, j)),
        out_shape=jax.ShapeDtypeStruct((depth, rows, n), F32),
        compiler_params=_params(3),
        name="ada_mod",
    )(c_pad, ada_w, ada_b.reshape(depth, 1, n))


def _norm_mod_kernel(x_ref, g_ref, sh_ref, sc_ref, o_ref):
    x = x_ref[...]
    y = x * lax.rsqrt(jnp.mean(x * x, axis=-1, keepdims=True) + RMS_EPS) * g_ref[...]
    o_ref[...] = (y * (1 + sc_ref[...]) + sh_ref[...]).astype(o_ref.dtype)


def _norm_mod(x, gains, gain_row, mod, row_of, j_shift, j_scale, seq, tm):
    m, d = x.shape
    per_seq = seq // tm
    return pl.pallas_call(
        _norm_mod_kernel,
        grid=(m // tm,),
        in_specs=[pl.BlockSpec((tm, d), lambda i: (i, 0)),
                  pl.BlockSpec((None, 1, d), lambda i: (gain_row, 0, 0)),
                  pl.BlockSpec((None, 1, d), lambda i: (row_of(i // per_seq, j_shift), 0, 0)),
                  pl.BlockSpec((None, 1, d), lambda i: (row_of(i // per_seq, j_scale), 0, 0))],
        out_specs=pl.BlockSpec((tm, d), lambda i: (i, 0)),
        out_shape=jax.ShapeDtypeStruct((m, d), BF16),
        compiler_params=_params(1),
        name="norm_mod",
    )(x, gains, mod, mod)


def _final_norm_kernel(x_ref, g_ref, o_ref):
    x = x_ref[...]
    o_ref[...] = x * lax.rsqrt(jnp.mean(x * x, axis=-1, keepdims=True) + RMS_EPS) * g_ref[...]


def _final_norm(x, gain, tm):
    m, d = x.shape
    return pl.pallas_call(
        _final_norm_kernel,
        grid=(m // tm,),
        in_specs=[pl.BlockSpec((tm, d), lambda i: (i, 0)),
                  pl.BlockSpec((1, d), lambda i: (0, 0))],
        out_specs=pl.BlockSpec((tm, d), lambda i: (i, 0)),
        out_shape=jax.ShapeDtypeStruct((m, d), F32),
        compiler_params=_params(1),
        name="final_norm",
    )(x, gain.reshape(1, d))


def _mm_plain_kernel(a_ref, w_ref, o_ref, wb_ref):
    @pl.when(pl.program_id(1) == 0)
    def _():
        wb_ref[...] = w_ref[...].astype(BF16)
    o_ref[...] = _dot(a_ref[...], wb_ref[...]).astype(o_ref.dtype)


def _mm_plain(a, w, layer, n, tm, tn, out_dtype):
    m, k = a.shape
    return pl.pallas_call(
        _mm_plain_kernel,
        grid=(n // tn, m // tm),
        in_specs=[pl.BlockSpec((tm, k), lambda j, i: (i, 0)),
                  pl.BlockSpec((None, k, tn), lambda j, i: (layer, 0, j))],
        out_specs=pl.BlockSpec((tm, tn), lambda j, i: (i, j)),
        out_shape=jax.ShapeDtypeStruct((m, n), out_dtype),
        scratch_shapes=[pltpu.VMEM((k, tn), BF16)],
        compiler_params=_params(2),
        name="mm_plain",
    )(a, w)


def _mm_swiglu_kernel(a_ref, wg_ref, wu_ref, o_ref, wgb_ref, wub_ref):
    @pl.when(pl.program_id(1) == 0)
    def _():
        wgb_ref[...] = wg_ref[...].astype(BF16)
        wub_ref[...] = wu_ref[...].astype(BF16)
    a = a_ref[...]
    g = _dot(a, wgb_ref[...])
    u = _dot(a, wub_ref[...])
    o_ref[...] = (g * jax.nn.sigmoid(g) * u).astype(o_ref.dtype)


def _mm_swiglu(a, w_in, layer, tm, tn):
    m, k = a.shape
    f = w_in.shape[2] // 2
    nb = f // tn
    return pl.pallas_call(
        _mm_swiglu_kernel,
        grid=(nb, m // tm),
        in_specs=[pl.BlockSpec((tm, k), lambda j, i: (i, 0)),
                  pl.BlockSpec((None, k, tn), lambda j, i: (layer, 0, j)),
                  pl.BlockSpec((None, k, tn), lambda j, i: (layer, 0, j + nb))],
        out_specs=pl.BlockSpec((tm, tn), lambda j, i: (i, j)),
        out_shape=jax.ShapeDtypeStruct((m, f), BF16),
        scratch_shapes=[pltpu.VMEM((k, tn), BF16), pltpu.VMEM((k, tn), BF16)],
        compiler_params=_params(2),
        name="mm_swiglu",
    )(a, w_in, w_in)


def _mm_resid_kernel(*refs, n_a, coef):
    a_refs = refs[:n_a]
    w_ref, x_ref, g_ref, o_ref, wb_ref = refs[n_a:]

    @pl.when(pl.program_id(1) == 0)
    def _():
        wb_ref[...] = w_ref[...].astype(BF16)
    acc = None
    k0 = 0
    for a_ref in a_refs:
        kw = a_ref.shape[1]
        part = _dot(a_ref[...], wb_ref[k0:k0 + kw, :])
        acc = part if acc is None else acc + part
        k0 += kw
    o_ref[...] = x_ref[...] + (coef * g_ref[...]) * acc


def _mm_resid(a_parts, w, layer, x, mod, row_of, j_gate, coef, seq, tm, tn, single_buffer_w):
    m = x.shape[0]
    k, n = w.shape[1], w.shape[2]
    per_seq = seq // tm
    w_mode = dict(pipeline_mode=pl.Buffered(1)) if single_buffer_w else {}
    return pl.pallas_call(
        functools.partial(_mm_resid_kernel, n_a=len(a_parts), coef=coef),
        grid=(n // tn, m // tm),
        in_specs=[pl.BlockSpec((tm, a.shape[1]), lambda j, i: (i, 0)) for a in a_parts] + [
                  pl.BlockSpec((None, k, tn), lambda j, i: (layer, 0, j), **w_mode),
                  pl.BlockSpec((tm, tn), lambda j, i: (i, j)),
                  pl.BlockSpec((None, 1, tn), lambda j, i: (row_of(i // per_seq, j_gate), 0, j))],
        out_specs=pl.BlockSpec((tm, tn), lambda j, i: (i, j)),
        out_shape=jax.ShapeDtypeStruct((m, n), F32),
        scratch_shapes=[pltpu.VMEM((k, tn), BF16)],
        compiler_params=_params(2),
        name="mm_resid",
    )(*a_parts, w, x, mod)


def _shift_rows(x, k, row):
    return jnp.where(row < k, 0.0, pltpu.roll(x, k, axis=0))


def _pool_kernel(p_ref, w_ref, s_ref, o_ref):
    g = pl.program_id(0)
    x = p_ref[...]
    row = lax.broadcasted_iota(jnp.int32, x.shape, 0)
    acc = x
    win_sum = None
    span = 1
    for gi, win in enumerate(POOL_WINDOWS):
        while span < win:
            acc = acc + _shift_rows(acc, span, row)
            span *= 2
        win_sum = acc if win_sum is None else jnp.where(g == gi, acc, win_sum)
    win = jnp.left_shift(POOL_WINDOWS[0], g)
    count = jnp.minimum(row + 1, win).astype(F32)
    pooled = (win_sum / count - x).astype(BF16)
    o_ref[...] = (_dot(pooled, w_ref[...].astype(BF16)) * s_ref[...]).astype(o_ref.dtype)


def _pool(proj, pool_w, pool_scale, layer, batch, seq):
    n_groups, cg = pool_w.shape[1], pool_w.shape[2]
    assert tuple(POOL_WINDOWS[0] << g for g in range(n_groups)) == POOL_WINDOWS
    m = proj.shape[0]
    return pl.pallas_call(
        _pool_kernel,
        grid=(n_groups, batch),
        in_specs=[pl.BlockSpec((seq, cg), lambda g, b: (b, g)),
                  pl.BlockSpec((None, None, cg, cg), lambda g, b: (layer, g, 0, 0)),
                  pl.BlockSpec((None, 1, cg), lambda g, b: (layer, 0, g))],
        out_specs=pl.BlockSpec((seq, cg), lambda g, b: (b, g)),
        out_shape=jax.ShapeDtypeStruct((m, n_groups * cg), BF16),
        compiler_params=_params(2),
        name="pool_mix",
    )(proj, pool_w, pool_scale)


def _prep_kernel(*refs, has_vres, lora_bounds, n_pairs, tiles_per_seq):
    if has_vres:
        (r_ref, k_ref, v_ref, lo_ref, rp_ref, kp_ref, vp_ref, lop_ref,
         mur_ref, muk_ref, muv_ref, mulo_ref, w0_ref, a0_ref, kk_ref, ka_ref, rk_ref,
         w2_ref, a2_ref, g2_ref, v0_ref, v2_ref, vf_ref,
         rt_ref, at_ref, kt_ref, bt_ref, kh_ref, bh_ref, vv_ref, gate_ref, bonus_ref, pc_ref) = refs
    else:
        (r_ref, k_ref, v_ref, lo_ref, rp_ref, kp_ref, vp_ref, lop_ref,
         mur_ref, muk_ref, muv_ref, mulo_ref, w0_ref, a0_ref, kk_ref, ka_ref, rk_ref,
         w2_ref, a2_ref, g2_ref,
         rt_ref, at_ref, kt_ref, bt_ref, kh_ref, bh_ref, vv_ref, gate_ref, bonus_ref, pc_ref) = refs
    first = (pl.program_id(0) % tiles_per_seq) == 0
    tm = r_ref.shape[0]

    def shift_mix(cur_ref, prev_ref, mu_ref):
        z = cur_ref[...]
        row = lax.broadcasted_iota(jnp.int32, z.shape, 0)
        prev_last = jnp.where(first, 0.0, prev_ref[SUBLANES - 1:SUBLANES, :])
        zp = jnp.where(row == 0, prev_last, pltpu.roll(z, 1, axis=0))
        return z + (zp - z) * mu_ref[...]

    r = shift_mix(r_ref, rp_ref, mur_ref)
    k = shift_mix(k_ref, kp_ref, muk_ref)
    v = shift_mix(v_ref, vp_ref, muv_ref)
    lo = shift_mix(lo_ref, lop_ref, mulo_ref)

    e_w, e_a, e_g = lora_bounds
    lane = lax.broadcasted_iota(jnp.int32, lo.shape, 1)
    act = jnp.where(lane < e_w, jnp.tanh(lo),
                    jnp.where((lane >= e_a) & (lane < e_g), jax.nn.sigmoid(lo), lo)).astype(BF16)
    dlog = -(w0_ref[...] + _dot(act, w2_ref[...].astype(BF16)))
    softplus = jnp.maximum(dlog, 0.0) + jnp.log1p(jnp.exp(-jnp.abs(dlog)))
    lw = -jnp.exp(-softplus - 0.5)
    iclr = jax.nn.sigmoid(a0_ref[...] + _dot(act, a2_ref[...].astype(BF16)))
    gate = _dot(act, g2_ref[...].astype(BF16))
    if has_vres:
        vf = jnp.concatenate([vf_ref[p] for p in range(n_pairs)], axis=1)
        v = v + (vf - v) * jax.nn.sigmoid(v0_ref[...] + _dot(act, v2_ref[...].astype(BF16)))

    ones = _head_ones(LANES)
    kk = k * kk_ref[...]
    kk = kk / jnp.maximum(jnp.sqrt(_head_sum(kk * kk, ones)), L2_EPS)
    km = k * (1 + (iclr - 1) * ka_ref[...])
    bonus = _head_sum(r * km * rk_ref[...], ones) * v
    a = -kk
    b = kk * iclr

    ri = lax.broadcasted_iota(jnp.int32, (tm, tm), 0)
    ci = lax.broadcasted_iota(jnp.int32, (tm, tm), 1)
    same = (ri // CHUNK) == (ci // CHUNK)
    sel = jnp.concatenate([(same & (ci <= ri)).astype(BF16),
                           (same & (ci > ri)).astype(BF16),
                           same.astype(BF16)], axis=0)
    h3, m3, l3 = _split3(lw)
    sums = _dot(sel, h3) + _dot(sel, m3) + _dot(sel, l3)
    cum, rest, tot = sums[:tm], sums[tm:2 * tm], sums[2 * tm:]
    p_in = jnp.exp(cum)
    p_ex = jnp.exp(cum - lw)
    p_inv = jnp.exp(-cum)
    p_rest = jnp.exp(rest)
    p_tot = jnp.exp(tot)

    outs = ((rt_ref, r * p_in), (at_ref, a * p_ex), (kt_ref, km * p_inv), (bt_ref, b * p_inv),
            (kh_ref, km * p_rest), (bh_ref, b * p_rest), (vv_ref, v), (gate_ref, gate),
            (bonus_ref, bonus))
    for ref, val in outs:
        for p in range(n_pairs):
            ref[p] = val[:, p * PAIR:(p + 1) * PAIR].astype(ref.dtype)
    for p in range(n_pairs):
        for c in range(tm // CHUNK):
            pc_ref[p, c] = p_tot[c * CHUNK:c * CHUNK + SUBLANES, p * PAIR:(p + 1) * PAIR]


def _prep(proj, lora_in, vfirst, rows, mats, lora_bounds, batch, seq, d_rwkv, lo_blk, tm):
    m = proj.shape[0]
    has_vres = vfirst is not None
    n_pairs = d_rwkv // PAIR
    tiles_per_seq = seq // tm
    sub = tm // SUBLANES

    def cur(width, col):
        return pl.BlockSpec((tm, width), lambda i: (i, col))

    def prev(width, col):
        return pl.BlockSpec((SUBLANES, width), lambda i: (jnp.maximum(i * sub - 1, 0), col))

    def row(width):
        return pl.BlockSpec((1, width), lambda i: (0, 0))

    in_specs = [cur(d_rwkv, 1), cur(d_rwkv, 2), cur(d_rwkv, 3), cur(lo_blk, 0),
                prev(d_rwkv, 1), prev(d_rwkv, 2), prev(d_rwkv, 3), prev(lo_blk, 0),
                row(d_rwkv), row(d_rwkv), row(d_rwkv), row(lo_blk)]
    args = [proj] * 3 + [lora_in] + [proj] * 3 + [lora_in]
    args += [rows["mu_r"], rows["mu_k"], rows["mu_v"], rows["mu_lo"]]
    for name in ("w0", "a0", "k_k", "k_a", "r_k"):
        in_specs.append(row(d_rwkv))
        args.append(rows[name])
    for name in ("w2", "a2", "g2"):
        in_specs.append(pl.BlockSpec((lo_blk, d_rwkv), lambda i: (0, 0)))
        args.append(mats[name])
    if has_vres:
        in_specs += [row(d_rwkv), pl.BlockSpec((lo_blk, d_rwkv), lambda i: (0, 0)),
                     pl.BlockSpec((None, n_pairs, tm, PAIR),
                                  lambda i: (i // tiles_per_seq, 0, i % tiles_per_seq, 0))]
        args += [rows["v0"], mats["v2"], vfirst]

    pair_spec = pl.BlockSpec((None, n_pairs, tm, PAIR),
                             lambda i: (i // tiles_per_seq, 0, i % tiles_per_seq, 0))
    def pair_shape(dtype):
        return jax.ShapeDtypeStruct((batch, n_pairs, seq, PAIR), dtype)
    pair_shapes = [pair_shape(BF16)] * 6 + [pair_shape(F32)] * 3
    cpt = tm // CHUNK
    pc_spec = pl.BlockSpec((None, n_pairs, cpt, SUBLANES, PAIR),
                           lambda i: (i // tiles_per_seq, 0, i % tiles_per_seq, 0, 0))
    pc_shape = jax.ShapeDtypeStruct((batch, n_pairs, seq // CHUNK, SUBLANES, PAIR), F32)
    return pl.pallas_call(
        functools.partial(_prep_kernel, has_vres=has_vres, lora_bounds=lora_bounds,
                          n_pairs=n_pairs, tiles_per_seq=tiles_per_seq),
        grid=(m // tm,),
        in_specs=in_specs,
        out_specs=[pair_spec] * 9 + [pc_spec],
        out_shape=pair_shapes + [pc_shape],
        compiler_params=_params(1),
        name="rwkv_prep",
    )(*args)


def _mm1(a, b, fn):
    return fn(a.astype(BF16), b.astype(BF16))


def _wkv_kernel(rt_ref, at_ref, kt_ref, bt_ref, kh_ref, bh_ref, v_ref, gate_ref, bonus_ref, pc_ref,
                lg_ref, lb_ref, o_ref, s_ref, y_ref):
    @pl.when(pl.program_id(2) == 0)
    def _():
        s_ref[...] = jnp.zeros_like(s_ref)

    pairs = rt_ref.shape[0]
    n_chunks = rt_ref.shape[1] // CHUNK
    lane = lax.broadcasted_iota(jnp.int32, (CHUNK, PAIR), 1)
    head0 = lane < HEAD_SIZE
    ri = lax.broadcasted_iota(jnp.int32, (PAIR, PAIR), 0)
    ci = lax.broadcasted_iota(jnp.int32, (PAIR, PAIR), 1)
    strict = ri > ci
    incl = ri >= ci
    eye = (ri == ci).astype(F32)

    def stack(x):
        return jnp.concatenate([jnp.where(head0, x, 0.0), jnp.where(head0, 0.0, x)], axis=0)

    def chunk(c, carry):
        sl = pl.ds(pl.multiple_of(c * CHUNK, CHUNK), CHUNK)
        ps = range(pairs)
        ar = [jnp.concatenate([stack(at_ref[p, sl, :]), stack(rt_ref[p, sl, :])], axis=0) for p in ps]
        bk = [jnp.concatenate([stack(bt_ref[p, sl, :]), stack(kt_ref[p, sl, :])], axis=0) for p in ps]
        v_s = [stack(v_ref[p, sl, :]) for p in ps]
        s = [s_ref[p] for p in ps]
        amat = [_mm1(ar[p], bk[p], _dot_nt) for p in ps]
        xs = [_mm1(ar[p], s[p], _dot_nt) for p in ps]
        a_ak = [jnp.where(strict, amat[p][:PAIR, PAIR:], 0.0) for p in ps]
        rhs = [xs[p][:PAIR] + _mm1(a_ak[p], v_s[p], _dot) for p in ps]

        pw = [jnp.where(strict, amat[p][:PAIR, :PAIR], 0.0) for p in ps]
        inv = [eye + pw[p] for p in ps]
        span = 1
        while 2 * span < CHUNK:
            pw = [_mm1(pw[p], pw[p], _dot) for p in ps]
            inv = [inv[p] + _mm1(pw[p], inv[p], _dot) for p in ps]
            span *= 2

        u_s = [_mm1(inv[p], rhs[p], _dot) for p in ps]
        uv = [jnp.concatenate([u_s[p], v_s[p]], axis=0) for p in ps]
        a_r = [jnp.where(jnp.concatenate([incl, incl], axis=1), amat[p][PAIR:], 0.0) for p in ps]
        y_s = [xs[p][PAIR:] + _mm1(a_r[p], uv[p], _dot) for p in ps]
        for p in ps:
            bkh = jnp.concatenate([stack(bh_ref[p, sl, :]), stack(kh_ref[p, sl, :])], axis=0)
            s_new = s[p] * pc_ref[p, c][0:1, :] + _mm1(uv[p], bkh, _dot_tn)
            y_ref[p, sl, :] = y_s[p][:CHUNK] + y_s[p][CHUNK:]
            s_ref[p] = s_new
        return carry

    lax.fori_loop(0, n_chunks, chunk, 0)

    ones = _head_ones(PAIR)
    for p in range(pairs):
        y = y_ref[p]
        mean = _head_sum(y, ones) * (1.0 / HEAD_SIZE)
        yc = y - mean
        var = _head_sum(yc * yc, ones) * (1.0 / HEAD_SIZE)
        yn = yc * lax.rsqrt(var + GN_EPS) * lg_ref[:, p * PAIR:(p + 1) * PAIR]
        yn = yn + lb_ref[:, p * PAIR:(p + 1) * PAIR] + bonus_ref[p]
        o_ref[:, p * PAIR:(p + 1) * PAIR] = (yn * gate_ref[p]).astype(o_ref.dtype)


def _wkv(prep_out, lnx_gain, lnx_bias, layer, batch, seq, tc, pairs_per_step):
    rt, at, kt, bt, kh, bh, vv, gate, bonus, pc = prep_out
    n_pairs = rt.shape[1]
    n_tc = seq // tc
    pb = pairs_per_step
    blk = pl.BlockSpec((None, pb, tc, PAIR), lambda b, p, c: (b, p, c, 0))
    pc_blk = pl.BlockSpec((None, pb, tc // CHUNK, SUBLANES, PAIR), lambda b, p, c: (b, p, c, 0, 0))
    row = pl.BlockSpec((None, 1, pb * PAIR), lambda b, p, c: (layer, 0, p))
    return pl.pallas_call(
        _wkv_kernel,
        grid=(batch, n_pairs // pb, n_tc),
        in_specs=[blk] * 9 + [pc_blk, row, row],
        out_specs=pl.BlockSpec((tc, pb * PAIR), lambda b, p, c: (b * n_tc + c, p)),
        out_shape=jax.ShapeDtypeStruct((batch * seq, n_pairs * PAIR), BF16),
        scratch_shapes=[pltpu.VMEM((pb, PAIR, PAIR), F32), pltpu.VMEM((pb, tc, PAIR), F32)],
        compiler_params=_params(3),
        name="wkv7",
    )(rt, at, kt, bt, kh, bh, vv, gate, bonus, pc, lnx_gain, lnx_bias)


def _tile(n, target):
    t = min(n, target)
    while n % t:
        t //= 2
    return t


def kernel(x, c, ada_w, ada_b, norm_gain, ffn1_w_in, ffn1_w_out, ffn2_w_in, ffn2_w_out, mix_w_in,
           mix_w_in_vres, shift_mu, shift_mu_vres, pool_w, pool_scale, decay_w0, decay_w2, iclr_a0,
           iclr_a2, gate_g2, vres_v0, vres_v2, k_k, k_a, r_k, lnx_gain, lnx_bias, mix_w_out,
           final_gain):
    batch, seq, d = x.shape
    depth = ada_w.shape[0]
    m = batch * seq
    d_pool = pool_scale.shape[1]
    d_rwkv = decay_w0.shape[1]
    d_w, d_a, d_g = decay_w2.shape[1], iclr_a2.shape[1], gate_g2.shape[1]
    d_v = vres_v2.shape[1]
    lo_blk = d_w + d_a + d_g + d_v
    d_main = d_pool + 3 * d_rwkv
    assert d_pool == d_rwkv and lo_blk % LANES == 0
    assert d_rwkv % PAIR == 0 and seq % (2 * CHUNK) == 0
    lora_bounds = (d_w, d_w + d_a, d_w + d_a + d_g)

    tm_big = _tile(seq, 1024)
    tm_mid = _tile(seq, 512)
    tm_norm = _tile(seq, 256)
    tm_prep = 2 * CHUNK
    tc = _tile(seq, 512)
    pairs_per_step = _tile(d_rwkv // PAIR, 8)

    c_pad = jnp.pad(c, ((0, SUBLANES - batch), (0, 0)))
    mod = _ada(c_pad, ada_w, ada_b, _tile(d, 256), _tile(N_MOD * d, N_MOD * 1024))
    mod = mod.reshape(depth * SUBLANES * N_MOD, 1, d)
    gains = norm_gain.reshape(depth * 3, 1, d)
    pool_scale3 = pool_scale.reshape(depth, 1, d_pool)
    lnx_gain3 = lnx_gain.reshape(depth, 1, d_rwkv)
    lnx_bias3 = lnx_bias.reshape(depth, 1, d_rwkv)

    xf = x.reshape(m, d)
    vfirst = None
    for l in range(depth):
        def row_of(b, j, l=l):
            return (l * SUBLANES + b) * N_MOD + j

        h = _norm_mod(xf, gains, 3 * l, mod, row_of, 0, 1, seq, tm_norm)
        act = _mm_swiglu(h, ffn1_w_in, l, tm_big, _tile(ffn1_w_in.shape[2] // 2, 256))
        xf = _mm_resid([act], ffn1_w_out, l, xf, mod, row_of, 2, 0.5, seq, tm_mid, _tile(d, 512), True)

        h = _norm_mod(xf, gains, 3 * l + 1, mod, row_of, 3, 4, seq, tm_norm)
        proj = _mm_plain(h, mix_w_in, l, d_main, tm_big, _tile(d_main, 512), F32)
        if l == 0:
            w_tail = jnp.zeros((d, d_v), F32)
            mu_tail = jnp.zeros((d_v,), F32)
        else:
            w_tail = mix_w_in_vres[l - 1]
            mu_tail = shift_mu_vres[l - 1]
        w_lora = jnp.concatenate([mix_w_in[l, :, d_main:], w_tail], axis=1)[None]
        lora_in = _mm_plain(h, w_lora, 0, lo_blk, tm_big, lo_blk, F32)
        mu = jnp.concatenate([shift_mu[l], mu_tail])

        pool_out = _pool(proj, pool_w, pool_scale3, l, batch, seq)

        def pad_rows(w, start):
            return jnp.pad(w, ((start, lo_blk - start - w.shape[0]), (0, 0)))

        rows = {"mu_r": mu[0:d_rwkv], "mu_k": mu[d_rwkv:2 * d_rwkv], "mu_v": mu[2 * d_rwkv:3 * d_rwkv],
                "mu_lo": mu[3 * d_rwkv:], "w0": decay_w0[l], "a0": iclr_a0[l], "k_k": k_k[l],
                "k_a": k_a[l], "r_k": r_k[l].reshape(-1)}
        mats = {"w2": pad_rows(decay_w2[l], 0), "a2": pad_rows(iclr_a2[l], d_w),
                "g2": pad_rows(gate_g2[l], d_w + d_a)}
        if l > 0:
            rows["v0"] = vres_v0[l - 1]
            mats["v2"] = pad_rows(vres_v2[l - 1], d_w + d_a + d_g)
        rows = {k_: v_.reshape(1, -1) for k_, v_ in rows.items()}
        prep_out = _prep(proj, lora_in, vfirst, rows, mats, lora_bounds, batch, seq, d_rwkv, lo_blk,
                         tm_prep)
        if l == 0:
            vfirst = prep_out[6]
        rwkv_out = _wkv(prep_out, lnx_gain3, lnx_bias3, l, batch, seq, tc, pairs_per_step)

        xf = _mm_resid([pool_out, rwkv_out], mix_w_out, l, xf, mod, row_of, 5, 1.0, seq, tm_big,
                       _tile(d, 512), False)

        h = _norm_mod(xf, gains, 3 * l + 2, mod, row_of, 6, 7, seq, tm_norm)
        act = _mm_swiglu(h, ffn2_w_in, l, tm_big, _tile(ffn2_w_in.shape[2] // 2, 256))
        xf = _mm_resid([act], ffn2_w_out, l, xf, mod, row_of, 8, 0.5, seq, tm_mid, _tile(d, 512), True)

    return _final_norm(xf, final_gain, tm_norm).reshape(batch, seq, d)
```

```python
import functools

import jax
import jax.numpy as jnp
from jax import lax
from jax.experimental import pallas as pl
from jax.experimental.pallas import tpu as pltpu

F32 = jnp.float32
BF16 = jnp.bfloat16

LANES = 128
SUBLANES = 8
HEAD_SIZE = 64
PAIR = 2 * HEAD_SIZE
CHUNK = 64
POOL_WINDOWS = (2, 4, 8, 16)
N_MOD = 9
RMS_EPS = 1e-6
GN_EPS = 64e-5
L2_EPS = 1e-12
VMEM_LIMIT_BYTES = 56 * 1024 * 1024


def _params(n_axes):
    return pltpu.CompilerParams(dimension_semantics=("arbitrary",) * n_axes,
                                vmem_limit_bytes=VMEM_LIMIT_BYTES)


def _dot(a, b):
    return jnp.dot(a, b, preferred_element_type=F32)


def _dot_nt(a, b):
    return lax.dot_general(a, b, (((1,), (1,)), ((), ())), preferred_element_type=F32)


def _dot_tn(a, b):
    return lax.dot_general(a, b, (((0,), (0,)), ((), ())), preferred_element_type=F32)


def _split2(x):
    hi = x.astype(BF16)
    lo = (x - hi.astype(F32)).astype(BF16)
    return hi, lo


def _split3(x):
    hi = x.astype(BF16)
    r1 = x - hi.astype(F32)
    mid = r1.astype(BF16)
    lo = (r1 - mid.astype(F32)).astype(BF16)
    return hi, mid, lo


def _head_ones(n):
    r = lax.broadcasted_iota(jnp.int32, (n, n), 0) // HEAD_SIZE
    c = lax.broadcasted_iota(jnp.int32, (n, n), 1) // HEAD_SIZE
    return (r == c).astype(BF16)


def _head_sum(x, ones):
    hi, lo = _split2(x)
    cols = []
    for j in range(x.shape[1] // LANES):
        sl = slice(j * LANES, (j + 1) * LANES)
        cols.append(_dot(hi[:, sl], ones) + _dot(lo[:, sl], ones))
    return cols[0] if len(cols) == 1 else jnp.concatenate(cols, axis=1)


def _ada_kernel(c_ref, w_ref, b_ref, o_ref):
    k = pl.program_id(2)
    c = c_ref[...]
    act = (c * jax.nn.sigmoid(c)).astype(BF16)
    part = _dot(act, w_ref[...].astype(BF16))

    @pl.when(k == 0)
    def _():
        o_ref[...] = part + b_ref[...]

    @pl.when(k > 0)
    def _():
        o_ref[...] += part


def _ada(c_pad, ada_w, ada_b, tk, tn):
    depth, d, n = ada_w.shape
    rows = c_pad.shape[0]
    return pl.pallas_call(
        _ada_kernel,
        grid=(depth, n // tn, d // tk),
        in_specs=[pl.BlockSpec((rows, tk), lambda l, j, k: (0, k)),
                  pl.BlockSpec((None, tk, tn), lambda l, j, k: (l, k, j)),
                  pl.BlockSpec((None, 1, tn), lambda l, j, k: (l, 0, j))],
        out_specs=pl.BlockSpec((None, rows, tn), lambda l, j, k: (l, 0, j)),
        out_shape=jax.ShapeDtypeStruct((depth, rows, n), F32),
        compiler_params=_params(3),
        name="ada_mod",
    )(c_pad, ada_w, ada_b.reshape(depth, 1, n))


def _norm_mod_kernel(x_ref, g_ref, sh_ref, sc_ref, o_ref):
    x = x_ref[...]
    y = x * lax.rsqrt(jnp.mean(x * x, axis=-1, keepdims=True) + RMS_EPS) * g_ref[...]
    o_ref[...] = (y * (1 + sc_ref[...]) + sh_ref[...]).astype(o_ref.dtype)


def _norm_mod(x, gains, gain_row, mod, row_of, j_shift, j_scale, seq, tm):
    m, d = x.shape
    per_seq = seq // tm
    return pl.pallas_call(
        _norm_mod_kernel,
        grid=(m // tm,),
        in_specs=[pl.BlockSpec((tm, d), lambda i: (i, 0)),
                  pl.BlockSpec((None, 1, d), lambda i: (gain_row, 0, 0)),
                  pl.BlockSpec((None, 1, d), lambda i: (row_of(i // per_seq, j_shift), 0, 0)),
                  pl.BlockSpec((None, 1, d), lambda i: (row_of(i // per_seq, j_scale), 0, 0))],
        out_specs=pl.BlockSpec((tm, d), lambda i: (i, 0)),
        out_shape=jax.ShapeDtypeStruct((m, d), BF16),
        compiler_params=_params(1),
        name="norm_mod",
    )(x, gains, mod, mod)


def _final_norm_kernel(x_ref, g_ref, o_ref):
    x = x_ref[...]
    o_ref[...] = x * lax.rsqrt(jnp.mean(x * x, axis=-1, keepdims=True) + RMS_EPS) * g_ref[...]


def _final_norm(x, gain, tm):
    m, d = x.shape
    return pl.pallas_call(
        _final_norm_kernel,
        grid=(m // tm,),
        in_specs=[pl.BlockSpec((tm, d), lambda i: (i, 0)),
                  pl.BlockSpec((1, d), lambda i: (0, 0))],
        out_specs=pl.BlockSpec((tm, d), lambda i: (i, 0)),
        out_shape=jax.ShapeDtypeStruct((m, d), F32),
        compiler_params=_params(1),
        name="final_norm",
    )(x, gain.reshape(1, d))


def _mm_plain_kernel(a_ref, wt_ref, o_ref, wb_ref):
    @pl.when(pl.program_id(1) == 0)
    def _():
        wb_ref[...] = wt_ref[...].T.astype(BF16)
    o_ref[...] = _dot(a_ref[...], wb_ref[...]).astype(o_ref.dtype)


def _mm_plain(a, wt, layer, n, tm, tn, out_dtype):
    m, k = a.shape
    return pl.pallas_call(
        _mm_plain_kernel,
        grid=(n // tn, m // tm),
        in_specs=[pl.BlockSpec((tm, k), lambda j, i: (i, 0)),
                  pl.BlockSpec((None, tn, k), lambda j, i: (layer, j, 0))],
        out_specs=pl.BlockSpec((tm, tn), lambda j, i: (i, j)),
        out_shape=jax.ShapeDtypeStruct((m, n), out_dtype),
        scratch_shapes=[pltpu.VMEM((k, tn), BF16)],
        compiler_params=_params(2),
        name="mm_plain",
    )(a, wt)


def _mm_swiglu_kernel(a_ref, wg_ref, wu_ref, o_ref, wgb_ref, wub_ref):
    @pl.when(pl.program_id(1) == 0)
    def _():
        wgb_ref[...] = wg_ref[...].astype(BF16)
        wub_ref[...] = wu_ref[...].astype(BF16)
    a = a_ref[...]
    g = _dot(a, wgb_ref[...])
    u = _dot(a, wub_ref[...])
    o_ref[...] = (g * jax.nn.sigmoid(g) * u).astype(o_ref.dtype)


def _mm_swiglu(a, w_in, layer, tm, tn):
    m, k = a.shape
    f = w_in.shape[2] // 2
    nb = f // tn
    return pl.pallas_call(
        _mm_swiglu_kernel,
        grid=(nb, m // tm),
        in_specs=[pl.BlockSpec((tm, k), lambda j, i: (i, 0)),
                  pl.BlockSpec((None, k, tn), lambda j, i: (layer, 0, j)),
                  pl.BlockSpec((None, k, tn), lambda j, i: (layer, 0, j + nb))],
        out_specs=pl.BlockSpec((tm, tn), lambda j, i: (i, j)),
        out_shape=jax.ShapeDtypeStruct((m, f), BF16),
        scratch_shapes=[pltpu.VMEM((k, tn), BF16), pltpu.VMEM((k, tn), BF16)],
        compiler_params=_params(2),
        name="mm_swiglu",
    )(a, w_in, w_in)


def _mm_resid_kernel(*refs, n_a, coef):
    a_refs = refs[:n_a]
    w_ref, x_ref, g_ref, o_ref, wb_ref = refs[n_a:]

    @pl.when(pl.program_id(1) == 0)
    def _():
        wb_ref[...] = w_ref[...].astype(BF16)
    acc = None
    k0 = 0
    for a_ref in a_refs:
        kw = a_ref.shape[1]
        part = _dot(a_ref[...], wb_ref[k0:k0 + kw, :])
        acc = part if acc is None else acc + part
        k0 += kw
    o_ref[...] = x_ref[...] + (coef * g_ref[...]) * acc


def _mm_resid(a_parts, w, layer, x, mod, row_of, j_gate, coef, seq, tm, tn, single_buffer_w):
    m = x.shape[0]
    k, n = w.shape[1], w.shape[2]
    per_seq = seq // tm
    w_mode = dict(pipeline_mode=pl.Buffered(1)) if single_buffer_w else {}
    return pl.pallas_call(
        functools.partial(_mm_resid_kernel, n_a=len(a_parts), coef=coef),
        grid=(n // tn, m // tm),
        in_specs=[pl.BlockSpec((tm, a.shape[1]), lambda j, i: (i, 0)) for a in a_parts] + [
                  pl.BlockSpec((None, k, tn), lambda j, i: (layer, 0, j), **w_mode),
                  pl.BlockSpec((tm, tn), lambda j, i: (i, j)),
                  pl.BlockSpec((None, 1, tn), lambda j, i: (row_of(i // per_seq, j_gate), 0, j))],
        out_specs=pl.BlockSpec((tm, tn), lambda j, i: (i, j)),
        out_shape=jax.ShapeDtypeStruct((m, n), F32),
        scratch_shapes=[pltpu.VMEM((k, tn), BF16)],
        compiler_params=_params(2),
        name="mm_resid",
    )(*a_parts, w, x, mod)


def _shift_rows(x, k, row):
    return jnp.where(row < k, 0.0, pltpu.roll(x, k, axis=0))


def _pool_kernel(p_ref, w_ref, s_ref, o_ref):
    g = pl.program_id(0)
    x = p_ref[...]
    row = lax.broadcasted_iota(jnp.int32, x.shape, 0)
    acc = x
    win_sum = None
    span = 1
    for gi, win in enumerate(POOL_WINDOWS):
        while span < win:
            acc = acc + _shift_rows(acc, span, row)
            span *= 2
        win_sum = acc if win_sum is None else jnp.where(g == gi, acc, win_sum)
    win = jnp.left_shift(POOL_WINDOWS[0], g)
    count = jnp.minimum(row + 1, win).astype(F32)
    pooled = (win_sum / count - x).astype(BF16)
    o_ref[...] = (_dot(pooled, w_ref[...].astype(BF16)) * s_ref[...]).astype(o_ref.dtype)


def _pool(proj, pool_w, pool_scale, layer, batch, seq):
    n_groups, cg = pool_w.shape[1], pool_w.shape[2]
    assert tuple(POOL_WINDOWS[0] << g for g in range(n_groups)) == POOL_WINDOWS
    m = proj.shape[0]
    return pl.pallas_call(
        _pool_kernel,
        grid=(n_groups, batch),
        in_specs=[pl.BlockSpec((seq, cg), lambda g, b: (b, g)),
                  pl.BlockSpec((None, None, cg, cg), lambda g, b: (layer, g, 0, 0)),
                  pl.BlockSpec((None, 1, cg), lambda g, b: (layer, 0, g))],
        out_specs=pl.BlockSpec((seq, cg), lambda g, b: (b, g)),
        out_shape=jax.ShapeDtypeStruct((m, n_groups * cg), BF16),
        compiler_params=_params(2),
        name="pool_mix",
    )(proj, pool_w, pool_scale)


def _prep_kernel(*refs, has_vres, lora_bounds, n_pairs, tiles_per_seq):
    if has_vres:
        (r_ref, k_ref, v_ref, lo_ref, rp_ref, kp_ref, vp_ref, lop_ref,
         mur_ref, muk_ref, muv_ref, mulo_ref, w0_ref, a0_ref, kk_ref, ka_ref, rk_ref,
         lw_ref, v0_ref, vf_ref,
         rt_ref, at_ref, kt_ref, bt_ref, kh_ref, bh_ref, vv_ref, gate_ref, bonus_ref, pc_ref,
         lwb_ref) = refs
    else:
        (r_ref, k_ref, v_ref, lo_ref, rp_ref, kp_ref, vp_ref, lop_ref,
         mur_ref, muk_ref, muv_ref, mulo_ref, w0_ref, a0_ref, kk_ref, ka_ref, rk_ref,
         lw_ref,
         rt_ref, at_ref, kt_ref, bt_ref, kh_ref, bh_ref, vv_ref, gate_ref, bonus_ref, pc_ref,
         lwb_ref) = refs

    @pl.when(pl.program_id(0) == 0)
    def _():
        lwb_ref[...] = lw_ref[...].astype(BF16)
    first = (pl.program_id(0) % tiles_per_seq) == 0
    tm = r_ref.shape[0]

    def shift_mix(cur_ref, prev_ref, mu_ref):
        z = cur_ref[...]
        row = lax.broadcasted_iota(jnp.int32, z.shape, 0)
        prev_last = jnp.where(first, 0.0, prev_ref[SUBLANES - 1:SUBLANES, :])
        zp = jnp.where(row == 0, prev_last, pltpu.roll(z, 1, axis=0))
        return z + (zp - z) * mu_ref[...]

    r = shift_mix(r_ref, rp_ref, mur_ref)
    k = shift_mix(k_ref, kp_ref, muk_ref)
    v = shift_mix(v_ref, vp_ref, muv_ref)
    lo = shift_mix(lo_ref, lop_ref, mulo_ref)

    e_w, e_a, e_g = lora_bounds
    lane = lax.broadcasted_iota(jnp.int32, lo.shape, 1)
    act = jnp.where(lane < e_w, jnp.tanh(lo),
                    jnp.where((lane >= e_a) & (lane < e_g), jax.nn.sigmoid(lo), lo)).astype(BF16)
    dlog = -(w0_ref[...] + _dot(act, lwb_ref[0]))
    softplus = jnp.maximum(dlog, 0.0) + jnp.log1p(jnp.exp(-jnp.abs(dlog)))
    lw = -jnp.exp(-softplus - 0.5)
    iclr = jax.nn.sigmoid(a0_ref[...] + _dot(act, lwb_ref[1]))
    gate = _dot(act, lwb_ref[2])
    if has_vres:
        vf = jnp.concatenate([vf_ref[p] for p in range(n_pairs)], axis=1)
        v = v + (vf - v) * jax.nn.sigmoid(v0_ref[...] + _dot(act, lwb_ref[3]))

    ones = _head_ones(LANES)
    kk = k * kk_ref[...]
    kk = kk / jnp.maximum(jnp.sqrt(_head_sum(kk * kk, ones)), L2_EPS)
    km = k * (1 + (iclr - 1) * ka_ref[...])
    bonus = _head_sum(r * km * rk_ref[...], ones) * v
    a = -kk
    b = kk * iclr

    ri = lax.broadcasted_iota(jnp.int32, (tm, tm), 0)
    ci = lax.broadcasted_iota(jnp.int32, (tm, tm), 1)
    same = (ri // CHUNK) == (ci // CHUNK)
    sel = jnp.concatenate([(same & (ci <= ri)).astype(BF16),
                           (same & (ci > ri)).astype(BF16),
                           same.astype(BF16)], axis=0)
    h3, m3, l3 = _split3(lw)
    sums = _dot(sel, h3) + _dot(sel, m3) + _dot(sel, l3)
    cum, rest, tot = sums[:tm], sums[tm:2 * tm], sums[2 * tm:]
    p_in = jnp.exp(cum)
    p_ex = jnp.exp(cum - lw)
    p_inv = jnp.exp(-cum)
    p_rest = jnp.exp(rest)
    p_tot = jnp.exp(tot)

    outs = ((rt_ref, r * p_in), (at_ref, a * p_ex), (kt_ref, km * p_inv), (bt_ref, b * p_inv),
            (kh_ref, km * p_rest), (bh_ref, b * p_rest), (vv_ref, v), (gate_ref, gate),
            (bonus_ref, bonus))
    for ref, val in outs:
        for p in range(n_pairs):
            ref[p] = val[:, p * PAIR:(p + 1) * PAIR].astype(ref.dtype)
    for p in range(n_pairs):
        for c in range(tm // CHUNK):
            pc_ref[p, c] = p_tot[c * CHUNK:c * CHUNK + SUBLANES, p * PAIR:(p + 1) * PAIR]


def _prep(proj, lora_in, vfirst, rows, mats, lora_bounds, batch, seq, d_rwkv, lo_blk, tm):
    m = proj.shape[0]
    has_vres = vfirst is not None
    n_pairs = d_rwkv // PAIR
    tiles_per_seq = seq // tm
    sub = tm // SUBLANES

    def cur(width, col):
        return pl.BlockSpec((tm, width), lambda i: (i, col))

    def prev(width, col):
        return pl.BlockSpec((SUBLANES, width), lambda i: (jnp.maximum(i * sub - 1, 0), col))

    def row(width):
        return pl.BlockSpec((1, width), lambda i: (0, 0))

    in_specs = [cur(d_rwkv, 1), cur(d_rwkv, 2), cur(d_rwkv, 3), cur(lo_blk, 0),
                prev(d_rwkv, 1), prev(d_rwkv, 2), prev(d_rwkv, 3), prev(lo_blk, 0),
                row(d_rwkv), row(d_rwkv), row(d_rwkv), row(lo_blk)]
    args = [proj] * 3 + [lora_in] + [proj] * 3 + [lora_in]
    args += [rows["mu_r"], rows["mu_k"], rows["mu_v"], rows["mu_lo"]]
    for name in ("w0", "a0", "k_k", "k_a", "r_k"):
        in_specs.append(row(d_rwkv))
        args.append(rows[name])
    in_specs.append(pl.BlockSpec(mats.shape, lambda i: (0, 0, 0), pipeline_mode=pl.Buffered(1)))
    args.append(mats)
    if has_vres:
        in_specs += [row(d_rwkv),
                     pl.BlockSpec((None, n_pairs, tm, PAIR),
                                  lambda i: (i // tiles_per_seq, 0, i % tiles_per_seq, 0))]
        args += [rows["v0"], vfirst]

    pair_spec = pl.BlockSpec((None, n_pairs, tm, PAIR),
                             lambda i: (i // tiles_per_seq, 0, i % tiles_per_seq, 0))
    def pair_shape(dtype):
        return jax.ShapeDtypeStruct((batch, n_pairs, seq, PAIR), dtype)
    pair_shapes = [pair_shape(BF16)] * 6 + [pair_shape(F32)] * 3
    cpt = tm // CHUNK
    pc_spec = pl.BlockSpec((None, n_pairs, cpt, SUBLANES, PAIR),
                           lambda i: (i // tiles_per_seq, 0, i % tiles_per_seq, 0, 0))
    pc_shape = jax.ShapeDtypeStruct((batch, n_pairs, seq // CHUNK, SUBLANES, PAIR), F32)
    return pl.pallas_call(
        functools.partial(_prep_kernel, has_vres=has_vres, lora_bounds=lora_bounds,
                          n_pairs=n_pairs, tiles_per_seq=tiles_per_seq),
        grid=(m // tm,),
        in_specs=in_specs,
        out_specs=[pair_spec] * 9 + [pc_spec],
        out_shape=pair_shapes + [pc_shape],
        scratch_shapes=[pltpu.VMEM(mats.shape, BF16)],
        compiler_params=_params(1),
        name="rwkv_prep",
    )(*args)


def _wkv_kernel(rt_ref, at_ref, kt_ref, bt_ref, kh_ref, bh_ref, v_ref, gate_ref, bonus_ref, pc_ref,
                lg_ref, lb_ref, o_ref, s_ref, y_ref):
    @pl.when(pl.program_id(2) == 0)
    def _():
        s_ref[...] = jnp.zeros_like(s_ref)

    pairs = rt_ref.shape[0]
    n_chunks = rt_ref.shape[1] // CHUNK
    head0 = lax.broadcasted_iota(jnp.int32, (CHUNK, PAIR), 1) < HEAD_SIZE
    row = lax.broadcasted_iota(jnp.int32, (CHUNK, 2 * PAIR), 0)
    col = lax.broadcasted_iota(jnp.int32, (CHUNK, 2 * PAIR), 1) % HEAD_SIZE
    strict = (row > col)[:, :PAIR]
    incl = row >= col
    eye = (row == col)[:, :PAIR].astype(F32)
    vi = lax.broadcasted_iota(jnp.int32, (PAIR, PAIR), 0) // HEAD_SIZE
    ki = lax.broadcasted_iota(jnp.int32, (PAIR, PAIR), 1) // HEAD_SIZE
    same_head = vi == ki

    def bd(x):
        x = x.astype(BF16)
        return jnp.concatenate([jnp.where(head0, x, 0.0), jnp.where(head0, 0.0, x)], axis=0)

    def mm(a, b_bd):
        return _dot(a.astype(BF16), b_bd)

    def chunk(c, carry):
        sl = pl.ds(pl.multiple_of(c * CHUNK, CHUNK), CHUNK)
        ps = range(pairs)
        ar = [jnp.concatenate([at_ref[p, sl, :], rt_ref[p, sl, :]], axis=0) for p in ps]
        bk = [jnp.concatenate([bd(bt_ref[p, sl, :]), bd(kt_ref[p, sl, :])], axis=0) for p in ps]
        v = [v_ref[p, sl, :].astype(BF16) for p in ps]
        v_bd = [bd(v[p]) for p in ps]
        s = [s_ref[p] for p in ps]
        amat = [_dot_nt(ar[p], bk[p]) for p in ps]
        xs = [_dot_nt(ar[p], s[p].astype(BF16)) for p in ps]
        a_ak = [jnp.where(strict, amat[p][:CHUNK, PAIR:], 0.0) for p in ps]
        rhs = [xs[p][:CHUNK] + mm(a_ak[p], v_bd[p]) for p in ps]

        pw = [jnp.where(strict, amat[p][:CHUNK, :PAIR], 0.0) for p in ps]
        inv = [eye + pw[p] for p in ps]
        pw = [mm(pw[p], bd(pw[p])) for p in ps]
        span = 2
        while 2 * span < CHUNK:
            both = [mm(jnp.concatenate([pw[p], inv[p]], axis=0), bd(pw[p])) for p in ps]
            inv = [inv[p] + both[p][CHUNK:] for p in ps]
            pw = [both[p][:CHUNK] for p in ps]
            span *= 2
        inv = [inv[p] + mm(inv[p], bd(pw[p])) for p in ps]

        u = [mm(inv[p], bd(rhs[p])).astype(BF16) for p in ps]
        a_r = [jnp.where(incl, amat[p][CHUNK:], 0.0) for p in ps]
        y = [xs[p][CHUNK:] + mm(a_r[p], jnp.concatenate([bd(u[p]), v_bd[p]], axis=0)) for p in ps]
        for p in ps:
            uv = jnp.concatenate([u[p], v[p]], axis=0)
            bkh = jnp.concatenate([bh_ref[p, sl, :], kh_ref[p, sl, :]], axis=0)
            upd = jnp.where(same_head, _dot_tn(uv, bkh), 0.0)
            y_ref[p, sl, :] = y[p]
            s_ref[p] = s[p] * pc_ref[p, c][0:1, :] + upd
        return carry

    lax.fori_loop(0, n_chunks, chunk, 0)

    ones = _head_ones(PAIR)
    for p in range(pairs):
        y = y_ref[p]
        mean = _head_sum(y, ones) * (1.0 / HEAD_SIZE)
        yc = y - mean
        var = _head_sum(yc * yc, ones) * (1.0 / HEAD_SIZE)
        yn = yc * lax.rsqrt(var + GN_EPS) * lg_ref[:, p * PAIR:(p + 1) * PAIR]
        yn = yn + lb_ref[:, p * PAIR:(p + 1) * PAIR] + bonus_ref[p]
        o_ref[:, p * PAIR:(p + 1) * PAIR] = (yn * gate_ref[p]).astype(o_ref.dtype)


def _wkv(prep_out, lnx_gain, lnx_bias, layer, batch, seq, tc, pairs_per_step):
    rt, at, kt, bt, kh, bh, vv, gate, bonus, pc = prep_out
    n_pairs = rt.shape[1]
    n_tc = seq // tc
    pb = pairs_per_step
    blk = pl.BlockSpec((None, pb, tc, PAIR), lambda b, p, c: (b, p, c, 0))
    pc_blk = pl.BlockSpec((None, pb, tc // CHUNK, SUBLANES, PAIR), lambda b, p, c: (b, p, c, 0, 0))
    row = pl.BlockSpec((None, 1, pb * PAIR), lambda b, p, c: (layer, 0, p))
    return pl.pallas_call(
        _wkv_kernel,
        grid=(batch, n_pairs // pb, n_tc),
        in_specs=[blk] * 9 + [pc_blk, row, row],
        out_specs=pl.BlockSpec((tc, pb * PAIR), lambda b, p, c: (b * n_tc + c, p)),
        out_shape=jax.ShapeDtypeStruct((batch * seq, n_pairs * PAIR), BF16),
        scratch_shapes=[pltpu.VMEM((pb, PAIR, PAIR), F32),
                        pltpu.VMEM((pb, tc, PAIR), F32)],
        compiler_params=_params(3),
        name="wkv7",
    )(rt, at, kt, bt, kh, bh, vv, gate, bonus, pc, lnx_gain, lnx_bias)


def _tile(n, target):
    t = min(n, target)
    while n % t:
        t //= 2
    return t


def kernel(x, c, ada_w, ada_b, norm_gain, ffn1_w_in, ffn1_w_out, ffn2_w_in, ffn2_w_out, mix_w_in,
           mix_w_in_vres, shift_mu, shift_mu_vres, pool_w, pool_scale, decay_w0, decay_w2, iclr_a0,
           iclr_a2, gate_g2, vres_v0, vres_v2, k_k, k_a, r_k, lnx_gain, lnx_bias, mix_w_out,
           final_gain):
    batch, seq, d = x.shape
    depth = ada_w.shape[0]
    m = batch * seq
    d_pool = pool_scale.shape[1]
    d_rwkv = decay_w0.shape[1]
    d_w, d_a, d_g = decay_w2.shape[1], iclr_a2.shape[1], gate_g2.shape[1]
    d_v = vres_v2.shape[1]
    lo_blk = d_w + d_a + d_g + d_v
    d_main = d_pool + 3 * d_rwkv
    assert d_pool == d_rwkv and lo_blk % LANES == 0
    assert d_rwkv % PAIR == 0 and seq % (2 * CHUNK) == 0
    lora_bounds = (d_w, d_w + d_a, d_w + d_a + d_g)

    tm_big = _tile(seq, 1024)
    tm_mid = _tile(seq, 512)
    tm_norm = _tile(seq, 256)
    tm_prep = 2 * CHUNK
    tc = _tile(seq, 256)
    pairs_per_step = _tile(d_rwkv // PAIR, 16)

    c_pad = jnp.pad(c, ((0, SUBLANES - batch), (0, 0)))
    mod = _ada(c_pad, ada_w, ada_b, _tile(d, 256), _tile(N_MOD * d, N_MOD * 1024))
    mod = mod.reshape(depth * SUBLANES * N_MOD, 1, d)
    gains = norm_gain.reshape(depth * 3, 1, d)
    pool_scale3 = pool_scale.reshape(depth, 1, d_pool)
    lnx_gain3 = lnx_gain.reshape(depth, 1, d_rwkv)
    lnx_bias3 = lnx_bias.reshape(depth, 1, d_rwkv)
    mix_w_in_t = jnp.swapaxes(mix_w_in, 1, 2)

    xf = x.reshape(m, d)
    vfirst = None
    for l in range(depth):
        def row_of(b, j, l=l):
            return (l * SUBLANES + b) * N_MOD + j

        h = _norm_mod(xf, gains, 3 * l, mod, row_of, 0, 1, seq, tm_norm)
        act = _mm_swiglu(h, ffn1_w_in, l, tm_big, _tile(ffn1_w_in.shape[2] // 2, 256))
        xf = _mm_resid([act], ffn1_w_out, l, xf, mod, row_of, 2, 0.5, seq, tm_mid, _tile(d, 512), True)

        h = _norm_mod(xf, gains, 3 * l + 1, mod, row_of, 3, 4, seq, tm_norm)
        proj = _mm_plain(h, mix_w_in_t, l, d_main, tm_big, _tile(d_main, 512), F32)
        if l == 0:
            w_tail_t = jnp.zeros((d_v, d), F32)
            mu_tail = jnp.zeros((d_v,), F32)
        else:
            w_tail_t = mix_w_in_vres[l - 1].T
            mu_tail = shift_mu_vres[l - 1]
        w_lora_t = jnp.concatenate([mix_w_in_t[l, d_main:], w_tail_t], axis=0)[None]
        lora_in = _mm_plain(h, w_lora_t, 0, lo_blk, tm_big, lo_blk, F32)
        mu = jnp.concatenate([shift_mu[l], mu_tail])

        pool_out = _pool(proj, pool_w, pool_scale3, l, batch, seq)

        def pad_rows(w, start):
            return jnp.pad(w, ((start, lo_blk - start - w.shape[0]), (0, 0)))

        rows = {"mu_r": mu[0:d_rwkv], "mu_k": mu[d_rwkv:2 * d_rwkv], "mu_v": mu[2 * d_rwkv:3 * d_rwkv],
                "mu_lo": mu[3 * d_rwkv:], "w0": decay_w0[l], "a0": iclr_a0[l], "k_k": k_k[l],
                "k_a": k_a[l], "r_k": r_k[l].reshape(-1)}
        mats = [pad_rows(decay_w2[l], 0), pad_rows(iclr_a2[l], d_w), pad_rows(gate_g2[l], d_w + d_a)]
        if l > 0:
            rows["v0"] = vres_v0[l - 1]
            mats.append(pad_rows(vres_v2[l - 1], d_w + d_a + d_g))
        mats = jnp.stack(mats)
        rows = {k_: v_.reshape(1, -1) for k_, v_ in rows.items()}
        prep_out = _prep(proj, lora_in, vfirst, rows, mats, lora_bounds, batch, seq, d_rwkv, lo_blk,
                         tm_prep)
        if l == 0:
            vfirst = prep_out[6]
        rwkv_out = _wkv(prep_out, lnx_gain3, lnx_bias3, l, batch, seq, tc, pairs_per_step)

        xf = _mm_resid([pool_out, rwkv_out], mix_w_out, l, xf, mod, row_of, 5, 1.0, seq, tm_big,
                       _tile(d, 512), False)

        h = _norm_mod(xf, gains, 3 * l + 2, mod, row_of, 6, 7, seq, tm_norm)
        act = _mm_swiglu(h, ffn2_w_in, l, tm_big, _tile(ffn2_w_in.shape[2] // 2, 256))
        xf = _mm_resid([act], ffn2_w_out, l, xf, mod, row_of, 8, 0.5, seq, tm_mid, _tile(d, 512), True)

    return _final_norm(xf, final_gain, tm_norm).reshape(batch, seq, d)
```

```python
import functools

import jax
import jax.numpy as jnp
from jax import lax
from jax.experimental import pallas as pl
from jax.experimental.pallas import tpu as pltpu

F32 = jnp.float32
BF16 = jnp.bfloat16

LANES = 128
SUBLANES = 8
HEAD_SIZE = 64
PAIR = 2 * HEAD_SIZE
CHUNK = 64
POOL_WINDOWS = (2, 4, 8, 16)
N_MOD = 9
RMS_EPS = 1e-6
GN_EPS = 64e-5
L2_EPS = 1e-12
DECAY_SCALE = 0.6065306597126334
VMEM_LIMIT_BYTES = 56 * 1024 * 1024


def _params(n_axes):
    return pltpu.CompilerParams(dimension_semantics=("arbitrary",) * n_axes,
                                vmem_limit_bytes=VMEM_LIMIT_BYTES)


def _dot(a, b):
    return jnp.dot(a, b, preferred_element_type=F32)


def _dot_nt(a, b):
    return lax.dot_general(a, b, (((1,), (1,)), ((), ())), preferred_element_type=F32)


def _dot_tn(a, b):
    return lax.dot_general(a, b, (((0,), (0,)), ((), ())), preferred_element_type=F32)


def _split2(x):
    hi = x.astype(BF16)
    lo = (x - hi.astype(F32)).astype(BF16)
    return hi, lo


def _split3(x):
    hi = x.astype(BF16)
    r1 = x - hi.astype(F32)
    mid = r1.astype(BF16)
    lo = (r1 - mid.astype(F32)).astype(BF16)
    return hi, mid, lo


def _head_ones(n):
    r = lax.broadcasted_iota(jnp.int32, (n, n), 0) // HEAD_SIZE
    c = lax.broadcasted_iota(jnp.int32, (n, n), 1) // HEAD_SIZE
    return (r == c).astype(BF16)


def _head_sum(x, ones):
    hi, lo = _split2(x)
    cols = []
    for j in range(x.shape[1] // LANES):
        sl = slice(j * LANES, (j + 1) * LANES)
        cols.append(_dot(hi[:, sl], ones) + _dot(lo[:, sl], ones))
    return cols[0] if len(cols) == 1 else jnp.concatenate(cols, axis=1)


def _ada_kernel(c_ref, w_ref, b_ref, o_ref):
    k = pl.program_id(2)
    c = c_ref[...]
    act = (c * jax.nn.sigmoid(c)).astype(BF16)
    part = _dot(act, w_ref[...].astype(BF16))

    @pl.when(k == 0)
    def _():
        o_ref[...] = part + b_ref[...]

    @pl.when(k > 0)
    def _():
        o_ref[...] += part


def _ada(c_pad, ada_w, ada_b, tk, tn):
    depth, d, n = ada_w.shape
    rows = c_pad.shape[0]
    return pl.pallas_call(
        _ada_kernel,
        grid=(depth, n // tn, d // tk),
        in_specs=[pl.BlockSpec((rows, tk), lambda l, j, k: (0, k)),
                  pl.BlockSpec((None, tk, tn), lambda l, j, k: (l, k, j)),
                  pl.BlockSpec((None, 1, tn), lambda l, j, k: (l, 0, j))],
        out_specs=pl.BlockSpec((None, rows, tn), lambda l, j, k: (l, 0, j)),
        out_shape=jax.ShapeDtypeStruct((depth, rows, n), F32),
        compiler_params=_params(3),
        name="ada_mod",
    )(c_pad, ada_w, ada_b.reshape(depth, 1, n))


def _norm_mod_kernel(x_ref, g_ref, sh_ref, sc_ref, o_ref):
    x = x_ref[...]
    y = x * lax.rsqrt(jnp.mean(x * x, axis=-1, keepdims=True) + RMS_EPS) * g_ref[...]
    o_ref[...] = (y * (1 + sc_ref[...]) + sh_ref[...]).astype(o_ref.dtype)


def _norm_mod(x, gains, gain_row, mod, row_of, j_shift, j_scale, seq, tm):
    m, d = x.shape
    per_seq = seq // tm
    return pl.pallas_call(
        _norm_mod_kernel,
        grid=(m // tm,),
        in_specs=[pl.BlockSpec((tm, d), lambda i: (i, 0)),
                  pl.BlockSpec((None, 1, d), lambda i: (gain_row, 0, 0)),
                  pl.BlockSpec((None, 1, d), lambda i: (row_of(i // per_seq, j_shift), 0, 0)),
                  pl.BlockSpec((None, 1, d), lambda i: (row_of(i // per_seq, j_scale), 0, 0))],
        out_specs=pl.BlockSpec((tm, d), lambda i: (i, 0)),
        out_shape=jax.ShapeDtypeStruct((m, d), BF16),
        compiler_params=_params(1),
        name="norm_mod",
    )(x, gains, mod, mod)


def _final_norm_kernel(x_ref, g_ref, o_ref):
    x = x_ref[...]
    o_ref[...] = x * lax.rsqrt(jnp.mean(x * x, axis=-1, keepdims=True) + RMS_EPS) * g_ref[...]


def _final_norm(x, gain, tm):
    m, d = x.shape
    return pl.pallas_call(
        _final_norm_kernel,
        grid=(m // tm,),
        in_specs=[pl.BlockSpec((tm, d), lambda i: (i, 0)),
                  pl.BlockSpec((1, d), lambda i: (0, 0))],
        out_specs=pl.BlockSpec((tm, d), lambda i: (i, 0)),
        out_shape=jax.ShapeDtypeStruct((m, d), F32),
        compiler_params=_params(1),
        name="final_norm",
    )(x, gain.reshape(1, d))


def _mm_plain_kernel(a_ref, wt_ref, o_ref, wb_ref):
    @pl.when(pl.program_id(1) == 0)
    def _():
        wb_ref[...] = wt_ref[...].T.astype(BF16)
    o_ref[...] = _dot(a_ref[...], wb_ref[...]).astype(o_ref.dtype)


def _mm_plain(a, wt, layer, n, tm, tn, out_dtype):
    m, k = a.shape
    return pl.pallas_call(
        _mm_plain_kernel,
        grid=(n // tn, m // tm),
        in_specs=[pl.BlockSpec((tm, k), lambda j, i: (i, 0)),
                  pl.BlockSpec((None, tn, k), lambda j, i: (layer, j, 0))],
        out_specs=pl.BlockSpec((tm, tn), lambda j, i: (i, j)),
        out_shape=jax.ShapeDtypeStruct((m, n), out_dtype),
        scratch_shapes=[pltpu.VMEM((k, tn), BF16)],
        compiler_params=_params(2),
        name="mm_plain",
    )(a, wt)


class _WeightTiles:
    def __init__(self, w_hbm, layer, wb_refs, stage_ref, sem, n_inner):
        self.w_hbm, self.layer = w_hbm, layer
        self.wb_refs, self.stage_ref, self.sem = wb_refs, stage_ref, sem
        k, self.tn = wb_refs[0].shape
        self.rows = k // n_inner
        self.n_inner = n_inner

    def _row_slice(self, chunk):
        start = chunk * self.rows
        if not isinstance(chunk, int):
            start = pl.multiple_of(start, self.rows)
        return pl.ds(start, self.rows)

    def _copy(self, tile, chunk, slot):
        src = self.w_hbm.at[self.layer, self._row_slice(chunk), pl.ds(tile * self.tn, self.tn)]
        return pltpu.make_async_copy(src, self.stage_ref.at[slot], self.sem.at[slot])

    def _land(self, tile, chunk, slot, dst_ref):
        self._copy(tile, chunk, slot).wait()
        dst_ref[self._row_slice(chunk), :] = self.stage_ref[slot].astype(BF16)

    def _load_first(self):
        self._copy(0, 0, 0).start()
        for chunk in range(self.n_inner):
            if chunk + 1 < self.n_inner:
                self._copy(0, chunk + 1, (chunk + 1) % 2).start()
            self._land(0, chunk, chunk % 2, self.wb_refs[0])

    def run(self, step):
        j, i = pl.program_id(0), pl.program_id(1)
        nxt = jnp.minimum(j + 1, pl.num_programs(0) - 1)

        @pl.when((j == 0) & (i == 0))
        def _():
            self._load_first()

        for parity in range(2):
            @pl.when(j % 2 == parity)
            def _(parity=parity):
                self._copy(nxt, i, 0).start()
                step(self.wb_refs[parity])
                self._land(nxt, i, 0, self.wb_refs[1 - parity])


def _weight_tile_scratch(k, tn, n_inner):
    assert k % n_inner == 0 and (k // n_inner) % (2 * SUBLANES) == 0
    return [pltpu.VMEM((k, tn), BF16), pltpu.VMEM((k, tn), BF16),
            pltpu.VMEM((2, k // n_inner, tn), F32), pltpu.SemaphoreType.DMA((2,))]


def _mm_swiglu_kernel(a_ref, wg_ref, wu_ref, o_ref, wgb_ref, wub_ref):
    @pl.when(pl.program_id(1) == 0)
    def _():
        wgb_ref[...] = wg_ref[...].astype(BF16)
        wub_ref[...] = wu_ref[...].astype(BF16)
    a = a_ref[...]
    g = _dot(a, wgb_ref[...])
    u = _dot(a, wub_ref[...])
    o_ref[...] = (g * jax.nn.sigmoid(g) * u).astype(o_ref.dtype)


def _mm_swiglu(a, w_in, layer, tm, tn):
    m, k = a.shape
    f = w_in.shape[2] // 2
    nb = f // tn
    return pl.pallas_call(
        _mm_swiglu_kernel,
        grid=(nb, m // tm),
        in_specs=[pl.BlockSpec((tm, k), lambda j, i: (i, 0)),
                  pl.BlockSpec((None, k, tn), lambda j, i: (layer, 0, j)),
                  pl.BlockSpec((None, k, tn), lambda j, i: (layer, 0, j + nb))],
        out_specs=pl.BlockSpec((tm, tn), lambda j, i: (i, j)),
        out_shape=jax.ShapeDtypeStruct((m, f), BF16),
        scratch_shapes=[pltpu.VMEM((k, tn), BF16), pltpu.VMEM((k, tn), BF16)],
        compiler_params=_params(2),
        name="mm_swiglu",
    )(a, w_in, w_in)


def _mm_resid_kernel(*refs, n_a, coef, layer, n_inner):
    a_refs = refs[:n_a]
    w_hbm, x_ref, g_ref, o_ref, wb0_ref, wb1_ref, stage_ref, sem = refs[n_a:]

    def step(wb_ref):
        acc = None
        k0 = 0
        for a_ref in a_refs:
            kw = a_ref.shape[1]
            part = _dot(a_ref[...], wb_ref[k0:k0 + kw, :])
            acc = part if acc is None else acc + part
            k0 += kw
        o_ref[...] = x_ref[...] + (coef * g_ref[...]) * acc

    _WeightTiles(w_hbm, layer, (wb0_ref, wb1_ref), stage_ref, sem, n_inner).run(step)


def _mm_resid(a_parts, w, layer, x, mod, row_of, j_gate, coef, seq, tm, tn):
    m = x.shape[0]
    k, n = w.shape[1], w.shape[2]
    per_seq = seq // tm
    n_inner = m // tm
    return pl.pallas_call(
        functools.partial(_mm_resid_kernel, n_a=len(a_parts), coef=coef, layer=layer,
                          n_inner=n_inner),
        grid=(n // tn, n_inner),
        in_specs=[pl.BlockSpec((tm, a.shape[1]), lambda j, i: (i, 0)) for a in a_parts] + [
                  pl.BlockSpec(memory_space=pl.ANY),
                  pl.BlockSpec((tm, tn), lambda j, i: (i, j)),
                  pl.BlockSpec((None, 1, tn), lambda j, i: (row_of(i // per_seq, j_gate), 0, j))],
        out_specs=pl.BlockSpec((tm, tn), lambda j, i: (i, j)),
        out_shape=jax.ShapeDtypeStruct((m, n), F32),
        scratch_shapes=_weight_tile_scratch(k, tn, n_inner),
        compiler_params=_params(2),
        name="mm_resid",
    )(*a_parts, w, x, mod)


def _shift_rows(x, k, row):
    return jnp.where(row < k, 0.0, pltpu.roll(x, k, axis=0))


def _pool_kernel(p_ref, w_ref, s_ref, o_ref):
    g = pl.program_id(0)
    x = p_ref[...]
    row = lax.broadcasted_iota(jnp.int32, x.shape, 0)
    acc = x
    win_sum = None
    span = 1
    for gi, win in enumerate(POOL_WINDOWS):
        while span < win:
            acc = acc + _shift_rows(acc, span, row)
            span *= 2
        win_sum = acc if win_sum is None else jnp.where(g == gi, acc, win_sum)
    win = jnp.left_shift(POOL_WINDOWS[0], g)
    count = jnp.minimum(row + 1, win).astype(F32)
    pooled = (win_sum / count - x).astype(BF16)
    o_ref[...] = (_dot(pooled, w_ref[...].astype(BF16)) * s_ref[...]).astype(o_ref.dtype)


def _pool(proj, pool_w, pool_scale, layer, batch, seq):
    n_groups, cg = pool_w.shape[1], pool_w.shape[2]
    assert tuple(POOL_WINDOWS[0] << g for g in range(n_groups)) == POOL_WINDOWS
    m = proj.shape[0]
    return pl.pallas_call(
        _pool_kernel,
        grid=(n_groups, batch),
        in_specs=[pl.BlockSpec((seq, cg), lambda g, b: (b, g)),
                  pl.BlockSpec((None, None, cg, cg), lambda g, b: (layer, g, 0, 0)),
                  pl.BlockSpec((None, 1, cg), lambda g, b: (layer, 0, g))],
        out_specs=pl.BlockSpec((seq, cg), lambda g, b: (b, g)),
        out_shape=jax.ShapeDtypeStruct((m, n_groups * cg), BF16),
        compiler_params=_params(2),
        name="pool_mix",
    )(proj, pool_w, pool_scale)


def _prep_kernel(*refs, has_vres, lora_bounds, n_pairs, tiles_per_seq):
    if has_vres:
        (r_ref, k_ref, v_ref, lo_ref, rp_ref, kp_ref, vp_ref, lop_ref,
         mur_ref, muk_ref, muv_ref, mulo_ref, w0_ref, a0_ref, kk_ref, ka_ref, rk_ref,
         lw_ref, v0_ref, vf_ref,
         rt_ref, at_ref, kt_ref, bt_ref, kh_ref, bh_ref, vv_ref, gate_ref, bonus_ref, pc_ref,
         lwb_ref) = refs
    else:
        (r_ref, k_ref, v_ref, lo_ref, rp_ref, kp_ref, vp_ref, lop_ref,
         mur_ref, muk_ref, muv_ref, mulo_ref, w0_ref, a0_ref, kk_ref, ka_ref, rk_ref,
         lw_ref,
         rt_ref, at_ref, kt_ref, bt_ref, kh_ref, bh_ref, vv_ref, gate_ref, bonus_ref, pc_ref,
         lwb_ref) = refs

    @pl.when(pl.program_id(0) == 0)
    def _():
        lwb_ref[...] = lw_ref[...].astype(BF16)
    first = (pl.program_id(0) % tiles_per_seq) == 0
    tm = r_ref.shape[0]

    def shift_mix(cur_ref, prev_ref, mu_ref):
        z = cur_ref[...]
        rolled = pltpu.roll(z, 1, axis=0)
        row = lax.broadcasted_iota(jnp.int32, (SUBLANES, z.shape[1]), 0)
        prev_last = jnp.where(first, 0.0, prev_ref[SUBLANES - 1:SUBLANES, :])
        top = jnp.where(row == 0, prev_last, rolled[:SUBLANES])
        zp = jnp.concatenate([top, rolled[SUBLANES:]], axis=0)
        return z + (zp - z) * mu_ref[...]

    r = shift_mix(r_ref, rp_ref, mur_ref)
    k = shift_mix(k_ref, kp_ref, muk_ref)
    v = shift_mix(v_ref, vp_ref, muv_ref)
    lo = shift_mix(lo_ref, lop_ref, mulo_ref)

    e_w, e_a, e_g = lora_bounds
    lane = lax.broadcasted_iota(jnp.int32, lo.shape, 1)
    act = jnp.where(lane < e_w, jnp.tanh(lo),
                    jnp.where((lane >= e_a) & (lane < e_g), jax.nn.sigmoid(lo), lo)).astype(BF16)
    lw = -DECAY_SCALE * jax.nn.sigmoid(w0_ref[...] + _dot(act, lwb_ref[0]))
    iclr = jax.nn.sigmoid(a0_ref[...] + _dot(act, lwb_ref[1]))
    gate = _dot(act, lwb_ref[2])
    if has_vres:
        vf = jnp.concatenate([vf_ref[p] for p in range(n_pairs)], axis=1)
        v = v + (vf - v) * jax.nn.sigmoid(v0_ref[...] + _dot(act, lwb_ref[3]))

    ones = _head_ones(LANES)
    kk = k * kk_ref[...]
    kk = kk * lax.rsqrt(jnp.maximum(_head_sum(kk * kk, ones), L2_EPS * L2_EPS))
    km = k * (1 + (iclr - 1) * ka_ref[...])
    bonus = _head_sum(r * km * rk_ref[...], ones) * v
    a = -kk
    b = kk * iclr

    ri = lax.broadcasted_iota(jnp.int32, (tm, tm), 0)
    ci = lax.broadcasted_iota(jnp.int32, (tm, tm), 1)
    same = (ri // CHUNK) == (ci // CHUNK)
    sel = jnp.concatenate([(same & (ci <= ri)).astype(BF16),
                           (same & (ci > ri)).astype(BF16),
                           same.astype(BF16)], axis=0)
    h3, m3, l3 = _split3(lw)
    sums = _dot(sel, h3) + _dot(sel, m3) + _dot(sel, l3)
    cum, rest, tot = sums[:tm], sums[tm:2 * tm], sums[2 * tm:]
    p_in = jnp.exp(cum)
    p_ex = jnp.exp(cum - lw)
    p_inv = jnp.exp(-cum)
    p_rest = jnp.exp(rest)
    p_tot = jnp.exp(tot)

    outs = ((rt_ref, r * p_in), (at_ref, a * p_ex), (kt_ref, km * p_inv), (bt_ref, b * p_inv),
            (kh_ref, km * p_rest), (bh_ref, b * p_rest), (vv_ref, v), (gate_ref, gate),
            (bonus_ref, bonus))
    for ref, val in outs:
        for p in range(n_pairs):
            ref[p] = val[:, p * PAIR:(p + 1) * PAIR].astype(ref.dtype)
    for p in range(n_pairs):
        for c in range(tm // CHUNK):
            pc_ref[p, c] = p_tot[c * CHUNK:c * CHUNK + SUBLANES, p * PAIR:(p + 1) * PAIR]


def _prep(proj, lora_in, vfirst, rows, mats, lora_bounds, batch, seq, d_rwkv, lo_blk, tm):
    m = proj.shape[0]
    has_vres = vfirst is not None
    n_pairs = d_rwkv // PAIR
    tiles_per_seq = seq // tm
    sub = tm // SUBLANES

    def cur(width, col):
        return pl.BlockSpec((tm, width), lambda i: (i, col))

    def prev(width, col):
        return pl.BlockSpec((SUBLANES, width), lambda i: (jnp.maximum(i * sub - 1, 0), col))

    def row(width):
        return pl.BlockSpec((1, width), lambda i: (0, 0))

    in_specs = [cur(d_rwkv, 1), cur(d_rwkv, 2), cur(d_rwkv, 3), cur(lo_blk, 0),
                prev(d_rwkv, 1), prev(d_rwkv, 2), prev(d_rwkv, 3), prev(lo_blk, 0),
                row(d_rwkv), row(d_rwkv), row(d_rwkv), row(lo_blk)]
    args = [proj] * 3 + [lora_in] + [proj] * 3 + [lora_in]
    args += [rows["mu_r"], rows["mu_k"], rows["mu_v"], rows["mu_lo"]]
    for name in ("w0", "a0", "k_k", "k_a", "r_k"):
        in_specs.append(row(d_rwkv))
        args.append(rows[name])
    in_specs.append(pl.BlockSpec(mats.shape, lambda i: (0, 0, 0), pipeline_mode=pl.Buffered(1)))
    args.append(mats)
    if has_vres:
        in_specs += [row(d_rwkv),
                     pl.BlockSpec((None, n_pairs, tm, PAIR),
                                  lambda i: (i // tiles_per_seq, 0, i % tiles_per_seq, 0))]
        args += [rows["v0"], vfirst]

    pair_spec = pl.BlockSpec((None, n_pairs, tm, PAIR),
                             lambda i: (i // tiles_per_seq, 0, i % tiles_per_seq, 0))
    def pair_shape(dtype):
        return jax.ShapeDtypeStruct((batch, n_pairs, seq, PAIR), dtype)
    pair_shapes = [pair_shape(BF16)] * 6 + [pair_shape(F32)] * 3
    cpt = tm // CHUNK
    pc_spec = pl.BlockSpec((None, n_pairs, cpt, SUBLANES, PAIR),
                           lambda i: (i // tiles_per_seq, 0, i % tiles_per_seq, 0, 0))
    pc_shape = jax.ShapeDtypeStruct((batch, n_pairs, seq // CHUNK, SUBLANES, PAIR), F32)
    return pl.pallas_call(
        functools.partial(_prep_kernel, has_vres=has_vres, lora_bounds=lora_bounds,
                          n_pairs=n_pairs, tiles_per_seq=tiles_per_seq),
        grid=(m // tm,),
        in_specs=in_specs,
        out_specs=[pair_spec] * 9 + [pc_spec],
        out_shape=pair_shapes + [pc_shape],
        scratch_shapes=[pltpu.VMEM(mats.shape, BF16)],
        compiler_params=_params(1),
        name="rwkv_prep",
    )(*args)


def _wkv_kernel(rt_ref, at_ref, kt_ref, bt_ref, kh_ref, bh_ref, v_ref, gate_ref, bonus_ref, pc_ref,
                lg_ref, lb_ref, o_ref, s_ref, y_ref):
    @pl.when(pl.program_id(2) == 0)
    def _():
        s_ref[...] = jnp.zeros_like(s_ref)

    pairs = rt_ref.shape[0]
    n_chunks = rt_ref.shape[1] // CHUNK
    head0 = lax.broadcasted_iota(jnp.int32, (CHUNK, PAIR), 1) < HEAD_SIZE
    row = lax.broadcasted_iota(jnp.int32, (CHUNK, 2 * PAIR), 0)
    col = lax.broadcasted_iota(jnp.int32, (CHUNK, 2 * PAIR), 1) % HEAD_SIZE
    strict = (row > col)[:, :PAIR]
    incl = row >= col
    eye = (row == col)[:, :PAIR].astype(F32)
    vi = lax.broadcasted_iota(jnp.int32, (PAIR, PAIR), 0) // HEAD_SIZE
    ki = lax.broadcasted_iota(jnp.int32, (PAIR, PAIR), 1) // HEAD_SIZE
    same_head = vi == ki

    def bd(x):
        x = x.astype(BF16)
        return jnp.concatenate([jnp.where(head0, x, 0.0), jnp.where(head0, 0.0, x)], axis=0)

    def mm(a, b_bd):
        return _dot(a.astype(BF16), b_bd)

    def chunk(c, carry):
        sl = pl.ds(pl.multiple_of(c * CHUNK, CHUNK), CHUNK)
        ps = range(pairs)
        ar = [jnp.concatenate([at_ref[p, sl, :], rt_ref[p, sl, :]], axis=0) for p in ps]
        bk = [jnp.concatenate([bd(bt_ref[p, sl, :]), bd(kt_ref[p, sl, :])], axis=0) for p in ps]
        v = [v_ref[p, sl, :].astype(BF16) for p in ps]
        v_bd = [bd(v[p]) for p in ps]
        s = [s_ref[p] for p in ps]
        amat = [_dot_nt(ar[p], bk[p]) for p in ps]
        xs = [_dot_nt(ar[p], s[p].astype(BF16)) for p in ps]
        a_ak = [jnp.where(strict, amat[p][:CHUNK, PAIR:], 0.0) for p in ps]
        rhs = [xs[p][:CHUNK] + mm(a_ak[p], v_bd[p]) for p in ps]

        pw = [jnp.where(strict, amat[p][:CHUNK, :PAIR], 0.0) for p in ps]
        inv = [eye + pw[p] for p in ps]
        pw = [mm(pw[p], bd(pw[p])) for p in ps]
        span = 2
        while 2 * span < CHUNK:
            both = [mm(jnp.concatenate([pw[p], inv[p]], axis=0), bd(pw[p])) for p in ps]
            inv = [inv[p] + both[p][CHUNK:] for p in ps]
            pw = [both[p][:CHUNK] for p in ps]
            span *= 2
        inv = [inv[p] + mm(inv[p], bd(pw[p])) for p in ps]

        u = [mm(inv[p], bd(rhs[p])).astype(BF16) for p in ps]
        a_r = [jnp.where(incl, amat[p][CHUNK:], 0.0) for p in ps]
        y = [xs[p][CHUNK:] + mm(a_r[p], jnp.concatenate([bd(u[p]), v_bd[p]], axis=0)) for p in ps]
        for p in ps:
            uv = jnp.concatenate([u[p], v[p]], axis=0)
            bkh = jnp.concatenate([bh_ref[p, sl, :], kh_ref[p, sl, :]], axis=0)
            upd = jnp.where(same_head, _dot_tn(uv, bkh), 0.0)
            y_ref[p, sl, :] = y[p]
            s_ref[p] = s[p] * pc_ref[p, c][0:1, :] + upd
        return carry

    lax.fori_loop(0, n_chunks, chunk, 0)

    ones = _head_ones(PAIR)
    for p in range(pairs):
        y = y_ref[p]
        mean = _head_sum(y, ones) * (1.0 / HEAD_SIZE)
        yc = y - mean
        var = _head_sum(yc * yc, ones) * (1.0 / HEAD_SIZE)
        yn = yc * lax.rsqrt(var + GN_EPS) * lg_ref[:, p * PAIR:(p + 1) * PAIR]
        yn = yn + lb_ref[:, p * PAIR:(p + 1) * PAIR] + bonus_ref[p]
        o_ref[:, p * PAIR:(p + 1) * PAIR] = (yn * gate_ref[p]).astype(o_ref.dtype)


def _wkv(prep_out, lnx_gain, lnx_bias, layer, batch, seq, tc, pairs_per_step):
    rt, at, kt, bt, kh, bh, vv, gate, bonus, pc = prep_out
    n_pairs = rt.shape[1]
    n_tc = seq // tc
    pb = pairs_per_step
    blk = pl.BlockSpec((None, pb, tc, PAIR), lambda b, p, c: (b, p, c, 0))
    pc_blk = pl.BlockSpec((None, pb, tc // CHUNK, SUBLANES, PAIR), lambda b, p, c: (b, p, c, 0, 0))
    row = pl.BlockSpec((None, 1, pb * PAIR), lambda b, p, c: (layer, 0, p))
    return pl.pallas_call(
        _wkv_kernel,
        grid=(batch, n_pairs // pb, n_tc),
        in_specs=[blk] * 9 + [pc_blk, row, row],
        out_specs=pl.BlockSpec((tc, pb * PAIR), lambda b, p, c: (b * n_tc + c, p)),
        out_shape=jax.ShapeDtypeStruct((batch * seq, n_pairs * PAIR), BF16),
        scratch_shapes=[pltpu.VMEM((pb, PAIR, PAIR), F32),
                        pltpu.VMEM((pb, tc, PAIR), F32)],
        compiler_params=_params(3),
        name="wkv7",
    )(rt, at, kt, bt, kh, bh, vv, gate, bonus, pc, lnx_gain, lnx_bias)


def _tile(n, target):
    t = min(n, target)
    while n % t:
        t //= 2
    return t


def kernel(x, c, ada_w, ada_b, norm_gain, ffn1_w_in, ffn1_w_out, ffn2_w_in, ffn2_w_out, mix_w_in,
           mix_w_in_vres, shift_mu, shift_mu_vres, pool_w, pool_scale, decay_w0, decay_w2, iclr_a0,
           iclr_a2, gate_g2, vres_v0, vres_v2, k_k, k_a, r_k, lnx_gain, lnx_bias, mix_w_out,
           final_gain):
    batch, seq, d = x.shape
    depth = ada_w.shape[0]
    m = batch * seq
    d_pool = pool_scale.shape[1]
    d_rwkv = decay_w0.shape[1]
    d_w, d_a, d_g = decay_w2.shape[1], iclr_a2.shape[1], gate_g2.shape[1]
    d_v = vres_v2.shape[1]
    lo_blk = d_w + d_a + d_g + d_v
    d_main = d_pool + 3 * d_rwkv
    assert d_pool == d_rwkv and lo_blk % LANES == 0
    assert d_rwkv % PAIR == 0 and seq % (2 * CHUNK) == 0
    lora_bounds = (d_w, d_w + d_a, d_w + d_a + d_g)

    tm_big = _tile(seq, 1024)
    tm_mid = _tile(seq, 512)
    tm_norm = _tile(seq, 512)
    tm_prep = 2 * CHUNK
    tc = _tile(seq, 256)
    pairs_per_step = _tile(d_rwkv // PAIR, 16)

    c_pad = jnp.pad(c, ((0, SUBLANES - batch), (0, 0)))
    mod = _ada(c_pad, ada_w, ada_b, _tile(d, 256), _tile(N_MOD * d, N_MOD * 1024))
    mod = mod.reshape(depth * SUBLANES * N_MOD, 1, d)
    gains = norm_gain.reshape(depth * 3, 1, d)
    pool_scale3 = pool_scale.reshape(depth, 1, d_pool)
    lnx_gain3 = lnx_gain.reshape(depth, 1, d_rwkv)
    lnx_bias3 = lnx_bias.reshape(depth, 1, d_rwkv)
    mix_w_in_t = jnp.swapaxes(mix_w_in, 1, 2)

    xf = x.reshape(m, d)
    vfirst = None
    for l in range(depth):
        def row_of(b, j, l=l):
            return (l * SUBLANES + b) * N_MOD + j

        h = _norm_mod(xf, gains, 3 * l, mod, row_of, 0, 1, seq, tm_norm)
        act = _mm_swiglu(h, ffn1_w_in, l, tm_big, _tile(ffn1_w_in.shape[2] // 2, 256))
        xf = _mm_resid([act], ffn1_w_out, l, xf, mod, row_of, 2, 0.5, seq, tm_mid, _tile(d, 512))

        h = _norm_mod(xf, gains, 3 * l + 1, mod, row_of, 3, 4, seq, tm_norm)
        proj = _mm_plain(h, mix_w_in_t, l, d_main, tm_big, _tile(d_main, 512), F32)
        if l == 0:
            w_tail_t = jnp.zeros((d_v, d), F32)
            mu_tail = jnp.zeros((d_v,), F32)
        else:
            w_tail_t = mix_w_in_vres[l - 1].T
            mu_tail = shift_mu_vres[l - 1]
        w_lora_t = jnp.concatenate([mix_w_in_t[l, d_main:], w_tail_t], axis=0)[None]
        lora_in = _mm_plain(h, w_lora_t, 0, lo_blk, tm_big, lo_blk, F32)
        mu = jnp.concatenate([shift_mu[l], mu_tail])

        pool_out = _pool(proj, pool_w, pool_scale3, l, batch, seq)

        def pad_rows(w, start):
            return jnp.pad(w, ((start, lo_blk - start - w.shape[0]), (0, 0)))

        rows = {"mu_r": mu[0:d_rwkv], "mu_k": mu[d_rwkv:2 * d_rwkv], "mu_v": mu[2 * d_rwkv:3 * d_rwkv],
                "mu_lo": mu[3 * d_rwkv:], "w0": decay_w0[l], "a0": iclr_a0[l], "k_k": k_k[l],
                "k_a": k_a[l], "r_k": r_k[l].reshape(-1)}
        mats = [pad_rows(decay_w2[l], 0), pad_rows(iclr_a2[l], d_w), pad_rows(gate_g2[l], d_w + d_a)]
        if l > 0:
            rows["v0"] = vres_v0[l - 1]
            mats.append(pad_rows(vres_v2[l - 1], d_w + d_a + d_g))
        mats = jnp.stack(mats)
        rows = {k_: v_.reshape(1, -1) for k_, v_ in rows.items()}
        prep_out = _prep(proj, lora_in, vfirst, rows, mats, lora_bounds, batch, seq, d_rwkv, lo_blk,
                         tm_prep)
        if l == 0:
            vfirst = prep_out[6]
        rwkv_out = _wkv(prep_out, lnx_gain3, lnx_bias3, l, batch, seq, tc, pairs_per_step)

        xf = _mm_resid([pool_out, rwkv_out], mix_w_out, l, xf, mod, row_of, 5, 1.0, seq, tm_big,
                       _tile(d, 512))

        h = _norm_mod(xf, gains, 3 * l + 2, mod, row_of, 6, 7, seq, tm_norm)
        act = _mm_swiglu(h, ffn2_w_in, l, tm_big, _tile(ffn2_w_in.shape[2] // 2, 256))
        xf = _mm_resid([act], ffn2_w_out, l, xf, mod, row_of, 8, 0.5, seq, tm_mid, _tile(d, 512))

    return _final_norm(xf, final_gain, tm_norm).reshape(batch, seq, d)
```

```python
import functools

import jax
import jax.numpy as jnp
from jax import lax
from jax.experimental import pallas as pl
from jax.experimental.pallas import tpu as pltpu

F32 = jnp.float32
BF16 = jnp.bfloat16

LANES = 128
SUBLANES = 8
HEAD_SIZE = 64
PAIR = 2 * HEAD_SIZE
CHUNK = 64
POOL_WINDOWS = (2, 4, 8, 16)
N_MOD = 9
RMS_EPS = 1e-6
GN_EPS = 64e-5
L2_EPS = 1e-12
DECAY_SCALE = 0.6065306597126334
VMEM_LIMIT_BYTES = 56 * 1024 * 1024


def _params(n_axes):
    return pltpu.CompilerParams(dimension_semantics=("arbitrary",) * n_axes,
                                vmem_limit_bytes=VMEM_LIMIT_BYTES)


def _dot(a, b):
    return jnp.dot(a, b, preferred_element_type=F32)


def _dot_nt(a, b):
    return lax.dot_general(a, b, (((1,), (1,)), ((), ())), preferred_element_type=F32)


def _dot_tn(a, b):
    return lax.dot_general(a, b, (((0,), (0,)), ((), ())), preferred_element_type=F32)


def _split2(x):
    hi = x.astype(BF16)
    lo = (x - hi.astype(F32)).astype(BF16)
    return hi, lo


def _split3(x):
    hi = x.astype(BF16)
    r1 = x - hi.astype(F32)
    mid = r1.astype(BF16)
    lo = (r1 - mid.astype(F32)).astype(BF16)
    return hi, mid, lo


def _head_ones(n):
    r = lax.broadcasted_iota(jnp.int32, (n, n), 0) // HEAD_SIZE
    c = lax.broadcasted_iota(jnp.int32, (n, n), 1) // HEAD_SIZE
    return (r == c).astype(BF16)


def _head_sum(x, ones):
    hi, lo = _split2(x)
    cols = []
    for j in range(x.shape[1] // LANES):
        sl = slice(j * LANES, (j + 1) * LANES)
        cols.append(_dot(hi[:, sl], ones) + _dot(lo[:, sl], ones))
    return cols[0] if len(cols) == 1 else jnp.concatenate(cols, axis=1)


def _ada_kernel(c_ref, w_ref, b_ref, o_ref):
    k = pl.program_id(2)
    c = c_ref[...]
    act = (c * jax.nn.sigmoid(c)).astype(BF16)
    part = _dot(act, w_ref[...].astype(BF16))

    @pl.when(k == 0)
    def _():
        o_ref[...] = part + b_ref[...]

    @pl.when(k > 0)
    def _():
        o_ref[...] += part


def _ada(c_pad, ada_w, ada_b, tk, tn):
    depth, d, n = ada_w.shape
    rows = c_pad.shape[0]
    return pl.pallas_call(
        _ada_kernel,
        grid=(depth, n // tn, d // tk),
        in_specs=[pl.BlockSpec((rows, tk), lambda l, j, k: (0, k)),
                  pl.BlockSpec((None, tk, tn), lambda l, j, k: (l, k, j)),
                  pl.BlockSpec((None, 1, tn), lambda l, j, k: (l, 0, j))],
        out_specs=pl.BlockSpec((None, rows, tn), lambda l, j, k: (l, 0, j)),
        out_shape=jax.ShapeDtypeStruct((depth, rows, n), F32),
        compiler_params=_params(3),
        name="ada_mod",
    )(c_pad, ada_w, ada_b.reshape(depth, 1, n))


def _norm_mod_kernel(x_ref, g_ref, sh_ref, sc_ref, o_ref):
    x = x_ref[...]
    y = x * lax.rsqrt(jnp.mean(x * x, axis=-1, keepdims=True) + RMS_EPS) * g_ref[...]
    o_ref[...] = (y * (1 + sc_ref[...]) + sh_ref[...]).astype(o_ref.dtype)


def _norm_mod(x, gains, gain_row, mod, row_of, j_shift, j_scale, seq, tm):
    m, d = x.shape
    per_seq = seq // tm
    return pl.pallas_call(
        _norm_mod_kernel,
        grid=(m // tm,),
        in_specs=[pl.BlockSpec((tm, d), lambda i: (i, 0)),
                  pl.BlockSpec((None, 1, d), lambda i: (gain_row, 0, 0)),
                  pl.BlockSpec((None, 1, d), lambda i: (row_of(i // per_seq, j_shift), 0, 0)),
                  pl.BlockSpec((None, 1, d), lambda i: (row_of(i // per_seq, j_scale), 0, 0))],
        out_specs=pl.BlockSpec((tm, d), lambda i: (i, 0)),
        out_shape=jax.ShapeDtypeStruct((m, d), BF16),
        compiler_params=_params(1),
        name="norm_mod",
    )(x, gains, mod, mod)


def _final_norm_kernel(x_ref, g_ref, o_ref):
    x = x_ref[...]
    o_ref[...] = x * lax.rsqrt(jnp.mean(x * x, axis=-1, keepdims=True) + RMS_EPS) * g_ref[...]


def _final_norm(x, gain, tm):
    m, d = x.shape
    return pl.pallas_call(
        _final_norm_kernel,
        grid=(m // tm,),
        in_specs=[pl.BlockSpec((tm, d), lambda i: (i, 0)),
                  pl.BlockSpec((1, d), lambda i: (0, 0))],
        out_specs=pl.BlockSpec((tm, d), lambda i: (i, 0)),
        out_shape=jax.ShapeDtypeStruct((m, d), F32),
        compiler_params=_params(1),
        name="final_norm",
    )(x, gain.reshape(1, d))


def _mm_plain_kernel(a_ref, wt_ref, o_ref, wb_ref):
    @pl.when(pl.program_id(1) == 0)
    def _():
        wb_ref[...] = wt_ref[...].T.astype(BF16)
    o_ref[...] = _dot(a_ref[...], wb_ref[...]).astype(o_ref.dtype)


def _mm_plain(a, wt, layer, n, tm, tn, out_dtype):
    m, k = a.shape
    return pl.pallas_call(
        _mm_plain_kernel,
        grid=(n // tn, m // tm),
        in_specs=[pl.BlockSpec((tm, k), lambda j, i: (i, 0)),
                  pl.BlockSpec((None, tn, k), lambda j, i: (layer, j, 0))],
        out_specs=pl.BlockSpec((tm, tn), lambda j, i: (i, j)),
        out_shape=jax.ShapeDtypeStruct((m, n), out_dtype),
        scratch_shapes=[pltpu.VMEM((k, tn), BF16)],
        compiler_params=_params(2),
        name="mm_plain",
    )(a, wt)


class _WeightTiles:
    def __init__(self, w_hbm, layer, wb_refs, stage_ref, sem, n_inner):
        self.w_hbm, self.layer = w_hbm, layer
        self.wb_refs, self.stage_ref, self.sem = wb_refs, stage_ref, sem
        k, self.tn = wb_refs[0].shape
        self.rows = k // n_inner
        self.n_inner = n_inner

    def _row_slice(self, chunk):
        start = chunk * self.rows
        if not isinstance(chunk, int):
            start = pl.multiple_of(start, self.rows)
        return pl.ds(start, self.rows)

    def _copy(self, tile, chunk, slot):
        src = self.w_hbm.at[self.layer, self._row_slice(chunk), pl.ds(tile * self.tn, self.tn)]
        return pltpu.make_async_copy(src, self.stage_ref.at[slot], self.sem.at[slot])

    def _land(self, tile, chunk, slot, dst_ref):
        self._copy(tile, chunk, slot).wait()
        dst_ref[self._row_slice(chunk), :] = self.stage_ref[slot].astype(BF16)

    def _load_first(self):
        self._copy(0, 0, 0).start()
        for chunk in range(self.n_inner):
            if chunk + 1 < self.n_inner:
                self._copy(0, chunk + 1, (chunk + 1) % 2).start()
            self._land(0, chunk, chunk % 2, self.wb_refs[0])

    def run(self, step):
        j, i = pl.program_id(0), pl.program_id(1)
        nxt = jnp.minimum(j + 1, pl.num_programs(0) - 1)

        @pl.when((j == 0) & (i == 0))
        def _():
            self._load_first()

        self._copy(nxt, i, 0).start(priority=1)
        for parity in range(2):
            @pl.when(j % 2 == parity)
            def _(parity=parity):
                step(self.wb_refs[parity])
        for parity in range(2):
            @pl.when(j % 2 == parity)
            def _(parity=parity):
                self._land(nxt, i, 0, self.wb_refs[1 - parity])


def _weight_tile_scratch(k, tn, n_inner):
    assert k % n_inner == 0 and (k // n_inner) % (2 * SUBLANES) == 0
    return [pltpu.VMEM((k, tn), BF16), pltpu.VMEM((k, tn), BF16),
            pltpu.VMEM((2, k // n_inner, tn), F32), pltpu.SemaphoreType.DMA((2,))]


def _mm_swiglu_kernel(a_ref, wg_ref, wu_ref, o_ref, wgb_ref, wub_ref):
    @pl.when(pl.program_id(1) == 0)
    def _():
        wgb_ref[...] = wg_ref[...].astype(BF16)
        wub_ref[...] = wu_ref[...].astype(BF16)
    a = a_ref[...]
    g = _dot(a, wgb_ref[...])
    u = _dot(a, wub_ref[...])
    o_ref[...] = (g * jax.nn.sigmoid(g) * u).astype(o_ref.dtype)


def _mm_swiglu(a, w_in, layer, tm, tn):
    m, k = a.shape
    f = w_in.shape[2] // 2
    nb = f // tn
    return pl.pallas_call(
        _mm_swiglu_kernel,
        grid=(nb, m // tm),
        in_specs=[pl.BlockSpec((tm, k), lambda j, i: (i, 0)),
                  pl.BlockSpec((None, k, tn), lambda j, i: (layer, 0, j)),
                  pl.BlockSpec((None, k, tn), lambda j, i: (layer, 0, j + nb))],
        out_specs=pl.BlockSpec((tm, tn), lambda j, i: (i, j)),
        out_shape=jax.ShapeDtypeStruct((m, f), BF16),
        scratch_shapes=[pltpu.VMEM((k, tn), BF16), pltpu.VMEM((k, tn), BF16)],
        compiler_params=_params(2),
        name="mm_swiglu",
    )(a, w_in, w_in)


def _mm_resid_kernel(*refs, n_a, coef, layer, n_inner):
    a_refs = refs[:n_a]
    w_hbm, x_ref, g_ref, o_ref, wb0_ref, wb1_ref, stage_ref, sem = refs[n_a:]

    def step(wb_ref):
        acc = None
        k0 = 0
        for a_ref in a_refs:
            kw = a_ref.shape[1]
            part = _dot(a_ref[...], wb_ref[k0:k0 + kw, :])
            acc = part if acc is None else acc + part
            k0 += kw
        o_ref[...] = x_ref[...] + (coef * g_ref[...]) * acc

    _WeightTiles(w_hbm, layer, (wb0_ref, wb1_ref), stage_ref, sem, n_inner).run(step)


def _mm_resid(a_parts, w, layer, x, mod, row_of, j_gate, coef, seq, tm, tn):
    m = x.shape[0]
    k, n = w.shape[1], w.shape[2]
    per_seq = seq // tm
    n_inner = m // tm
    return pl.pallas_call(
        functools.partial(_mm_resid_kernel, n_a=len(a_parts), coef=coef, layer=layer,
                          n_inner=n_inner),
        grid=(n // tn, n_inner),
        in_specs=[pl.BlockSpec((tm, a.shape[1]), lambda j, i: (i, 0)) for a in a_parts] + [
                  pl.BlockSpec(memory_space=pl.ANY),
                  pl.BlockSpec((tm, tn), lambda j, i: (i, j)),
                  pl.BlockSpec((None, 1, tn), lambda j, i: (row_of(i // per_seq, j_gate), 0, j))],
        out_specs=pl.BlockSpec((tm, tn), lambda j, i: (i, j)),
        out_shape=jax.ShapeDtypeStruct((m, n), F32),
        scratch_shapes=_weight_tile_scratch(k, tn, n_inner),
        compiler_params=_params(2),
        name="mm_resid",
    )(*a_parts, w, x, mod)


def _shift_rows(x, k, row):
    return jnp.where(row < k, 0.0, pltpu.roll(x, k, axis=0))


def _pool_kernel(p_ref, w_ref, s_ref, o_ref):
    g = pl.program_id(0)
    x = p_ref[...]
    row = lax.broadcasted_iota(jnp.int32, x.shape, 0)
    acc = x
    win_sum = None
    span = 1
    for gi, win in enumerate(POOL_WINDOWS):
        while span < win:
            acc = acc + _shift_rows(acc, span, row)
            span *= 2
        win_sum = acc if win_sum is None else jnp.where(g == gi, acc, win_sum)
    win = jnp.left_shift(POOL_WINDOWS[0], g)
    count = jnp.minimum(row + 1, win).astype(F32)
    pooled = (win_sum / count - x).astype(BF16)
    o_ref[...] = (_dot(pooled, w_ref[...].astype(BF16)) * s_ref[...]).astype(o_ref.dtype)


def _pool(proj, pool_w, pool_scale, layer, batch, seq):
    n_groups, cg = pool_w.shape[1], pool_w.shape[2]
    assert tuple(POOL_WINDOWS[0] << g for g in range(n_groups)) == POOL_WINDOWS
    m = proj.shape[0]
    return pl.pallas_call(
        _pool_kernel,
        grid=(n_groups, batch),
        in_specs=[pl.BlockSpec((seq, cg), lambda g, b: (b, g)),
                  pl.BlockSpec((None, None, cg, cg), lambda g, b: (layer, g, 0, 0)),
                  pl.BlockSpec((None, 1, cg), lambda g, b: (layer, 0, g))],
        out_specs=pl.BlockSpec((seq, cg), lambda g, b: (b, g)),
        out_shape=jax.ShapeDtypeStruct((m, n_groups * cg), BF16),
        compiler_params=_params(2),
        name="pool_mix",
    )(proj, pool_w, pool_scale)


def _prep_kernel(*refs, has_vres, lora_bounds, n_pairs, tiles_per_seq):
    if has_vres:
        (r_ref, k_ref, v_ref, lo_ref, rp_ref, kp_ref, vp_ref, lop_ref,
         mur_ref, muk_ref, muv_ref, mulo_ref, w0_ref, a0_ref, kk_ref, ka_ref, rk_ref,
         lw_ref, v0_ref, vf_ref,
         rt_ref, at_ref, kt_ref, bt_ref, kh_ref, bh_ref, vv_ref, gate_ref, bonus_ref, pc_ref,
         lwb_ref) = refs
    else:
        (r_ref, k_ref, v_ref, lo_ref, rp_ref, kp_ref, vp_ref, lop_ref,
         mur_ref, muk_ref, muv_ref, mulo_ref, w0_ref, a0_ref, kk_ref, ka_ref, rk_ref,
         lw_ref,
         rt_ref, at_ref, kt_ref, bt_ref, kh_ref, bh_ref, vv_ref, gate_ref, bonus_ref, pc_ref,
         lwb_ref) = refs

    @pl.when(pl.program_id(0) == 0)
    def _():
        lwb_ref[...] = lw_ref[...].astype(BF16)
    first = (pl.program_id(0) % tiles_per_seq) == 0
    tm = r_ref.shape[0]

    def shift_mix(cur_ref, prev_ref, mu_ref, cols):
        z = cur_ref[:, cols]
        rolled = pltpu.roll(z, 1, axis=0)
        row = lax.broadcasted_iota(jnp.int32, (SUBLANES, z.shape[1]), 0)
        prev_last = jnp.where(first, 0.0, prev_ref[SUBLANES - 1:SUBLANES, cols])
        top = jnp.where(row == 0, prev_last, rolled[:SUBLANES])
        zp = jnp.concatenate([top, rolled[SUBLANES:]], axis=0)
        return z + (zp - z) * mu_ref[:, cols]

    lo = shift_mix(lo_ref, lop_ref, mulo_ref, slice(None))
    e_w, e_a, e_g = lora_bounds
    lane = lax.broadcasted_iota(jnp.int32, lo.shape, 1)
    act = jnp.where(lane < e_w, jnp.tanh(lo),
                    jnp.where((lane >= e_a) & (lane < e_g), jax.nn.sigmoid(lo), lo)).astype(BF16)

    ri = lax.broadcasted_iota(jnp.int32, (tm, tm), 0)
    ci = lax.broadcasted_iota(jnp.int32, (tm, tm), 1)
    same = (ri // CHUNK) == (ci // CHUNK)
    sel = jnp.concatenate([(same & (ci <= ri)).astype(BF16),
                           (same & (ci > ri)).astype(BF16),
                           same.astype(BF16)], axis=0)
    ones = _head_ones(PAIR)

    lw_all = -DECAY_SCALE * jax.nn.sigmoid(w0_ref[...] + _dot(act, lwb_ref[0]))
    iclr_all = jax.nn.sigmoid(a0_ref[...] + _dot(act, lwb_ref[1]))
    gate_all = _dot(act, lwb_ref[2])
    if has_vres:
        mix_all = jax.nn.sigmoid(v0_ref[...] + _dot(act, lwb_ref[3]))

    for p in range(n_pairs):
        cols = slice(p * PAIR, (p + 1) * PAIR)
        r = shift_mix(r_ref, rp_ref, mur_ref, cols)
        k = shift_mix(k_ref, kp_ref, muk_ref, cols)
        v = shift_mix(v_ref, vp_ref, muv_ref, cols)
        lw = lw_all[:, cols]
        iclr = iclr_all[:, cols]
        gate_ref[p] = gate_all[:, cols]
        if has_vres:
            v = v + (vf_ref[p] - v) * mix_all[:, cols]
        vv_ref[p] = v

        kk = k * kk_ref[:, cols]
        kk = kk * lax.rsqrt(jnp.maximum(_head_sum(kk * kk, ones), L2_EPS * L2_EPS))
        km = k * (1 + (iclr - 1) * ka_ref[:, cols])
        bonus_ref[p] = _head_sum(r * km * rk_ref[:, cols], ones) * v
        a = -kk
        b = kk * iclr

        h3, m3, l3 = _split3(lw)
        sums = _dot(sel, h3) + _dot(sel, m3) + _dot(sel, l3)
        cum, rest, tot = sums[:tm], sums[tm:2 * tm], sums[2 * tm:]
        p_inv = jnp.exp(-cum)
        p_rest = jnp.exp(rest)
        rt_ref[p] = (r * jnp.exp(cum)).astype(BF16)
        at_ref[p] = (a * jnp.exp(cum - lw)).astype(BF16)
        kt_ref[p] = (km * p_inv).astype(BF16)
        bt_ref[p] = (b * p_inv).astype(BF16)
        kh_ref[p] = (km * p_rest).astype(BF16)
        bh_ref[p] = (b * p_rest).astype(BF16)
        p_tot = jnp.exp(tot)
        for c in range(tm // CHUNK):
            pc_ref[p, c] = p_tot[c * CHUNK:c * CHUNK + SUBLANES]


def _prep(proj, lora_in, vfirst, rows, mats, lora_bounds, batch, seq, d_rwkv, lo_blk, tm):
    m = proj.shape[0]
    has_vres = vfirst is not None
    n_pairs = d_rwkv // PAIR
    tiles_per_seq = seq // tm
    sub = tm // SUBLANES

    def cur(width, col):
        return pl.BlockSpec((tm, width), lambda i: (i, col))

    def prev(width, col):
        return pl.BlockSpec((SUBLANES, width), lambda i: (jnp.maximum(i * sub - 1, 0), col))

    def row(width):
        return pl.BlockSpec((1, width), lambda i: (0, 0))

    in_specs = [cur(d_rwkv, 1), cur(d_rwkv, 2), cur(d_rwkv, 3), cur(lo_blk, 0),
                prev(d_rwkv, 1), prev(d_rwkv, 2), prev(d_rwkv, 3), prev(lo_blk, 0),
                row(d_rwkv), row(d_rwkv), row(d_rwkv), row(lo_blk)]
    args = [proj] * 3 + [lora_in] + [proj] * 3 + [lora_in]
    args += [rows["mu_r"], rows["mu_k"], rows["mu_v"], rows["mu_lo"]]
    for name in ("w0", "a0", "k_k", "k_a", "r_k"):
        in_specs.append(row(d_rwkv))
        args.append(rows[name])
    in_specs.append(pl.BlockSpec(mats.shape, lambda i: (0, 0, 0), pipeline_mode=pl.Buffered(1)))
    args.append(mats)
    if has_vres:
        in_specs += [row(d_rwkv),
                     pl.BlockSpec((None, n_pairs, tm, PAIR),
                                  lambda i: (i // tiles_per_seq, 0, i % tiles_per_seq, 0))]
        args += [rows["v0"], vfirst]

    pair_spec = pl.BlockSpec((None, n_pairs, tm, PAIR),
                             lambda i: (i // tiles_per_seq, 0, i % tiles_per_seq, 0))
    def pair_shape(dtype):
        return jax.ShapeDtypeStruct((batch, n_pairs, seq, PAIR), dtype)
    pair_shapes = [pair_shape(BF16)] * 6 + [pair_shape(F32)] * 3
    cpt = tm // CHUNK
    pc_spec = pl.BlockSpec((None, n_pairs, cpt, SUBLANES, PAIR),
                           lambda i: (i // tiles_per_seq, 0, i % tiles_per_seq, 0, 0))
    pc_shape = jax.ShapeDtypeStruct((batch, n_pairs, seq // CHUNK, SUBLANES, PAIR), F32)
    return pl.pallas_call(
        functools.partial(_prep_kernel, has_vres=has_vres, lora_bounds=lora_bounds,
                          n_pairs=n_pairs, tiles_per_seq=tiles_per_seq),
        grid=(m // tm,),
        in_specs=in_specs,
        out_specs=[pair_spec] * 9 + [pc_spec],
        out_shape=pair_shapes + [pc_shape],
        scratch_shapes=[pltpu.VMEM(mats.shape, BF16)],
        compiler_params=_params(1),
        name="rwkv_prep",
    )(*args)


def _wkv_kernel(rt_ref, at_ref, kt_ref, bt_ref, kh_ref, bh_ref, v_ref, gate_ref, bonus_ref, pc_ref,
                lg_ref, lb_ref, o_ref, s_ref, y_ref):
    @pl.when(pl.program_id(2) == 0)
    def _():
        s_ref[...] = jnp.zeros_like(s_ref)

    pairs = rt_ref.shape[0]
    n_chunks = rt_ref.shape[1] // CHUNK
    head0 = lax.broadcasted_iota(jnp.int32, (CHUNK, PAIR), 1) < HEAD_SIZE
    row = lax.broadcasted_iota(jnp.int32, (CHUNK, 2 * PAIR), 0)
    col = lax.broadcasted_iota(jnp.int32, (CHUNK, 2 * PAIR), 1) % HEAD_SIZE
    strict = (row > col)[:, :PAIR]
    incl = row >= col
    eye = (row == col)[:, :PAIR].astype(F32)
    vi = lax.broadcasted_iota(jnp.int32, (PAIR, PAIR), 0) // HEAD_SIZE
    ki = lax.broadcasted_iota(jnp.int32, (PAIR, PAIR), 1) // HEAD_SIZE
    same_head = vi == ki

    def bd(x):
        x = x.astype(BF16)
        return jnp.concatenate([jnp.where(head0, x, 0.0), jnp.where(head0, 0.0, x)], axis=0)

    def mm(a, b_bd):
        return _dot(a.astype(BF16), b_bd)

    def chunk(c, carry):
        sl = pl.ds(pl.multiple_of(c * CHUNK, CHUNK), CHUNK)
        ps = range(pairs)
        ar = [jnp.concatenate([at_ref[p, sl, :], rt_ref[p, sl, :]], axis=0) for p in ps]
        bk = [jnp.concatenate([bd(bt_ref[p, sl, :]), bd(kt_ref[p, sl, :])], axis=0) for p in ps]
        v = [v_ref[p, sl, :].astype(BF16) for p in ps]
        v_bd = [bd(v[p]) for p in ps]
        s = [s_ref[p] for p in ps]
        amat = [_dot_nt(ar[p], bk[p]) for p in ps]
        xs = [_dot_nt(ar[p], s[p].astype(BF16)) for p in ps]
        a_ak = [jnp.where(strict, amat[p][:CHUNK, PAIR:], 0.0) for p in ps]
        rhs = [xs[p][:CHUNK] + mm(a_ak[p], v_bd[p]) for p in ps]

        pw = [jnp.where(strict, amat[p][:CHUNK, :PAIR], 0.0) for p in ps]
        inv = [eye + pw[p] for p in ps]
        pw = [mm(pw[p], bd(pw[p])) for p in ps]
        span = 2
        while 2 * span < CHUNK:
            both = [mm(jnp.concatenate([pw[p], inv[p]], axis=0), bd(pw[p])) for p in ps]
            inv = [inv[p] + both[p][CHUNK:] for p in ps]
            pw = [both[p][:CHUNK] for p in ps]
            span *= 2
        inv = [inv[p] + mm(inv[p], bd(pw[p])) for p in ps]

        u = [mm(inv[p], bd(rhs[p])).astype(BF16) for p in ps]
        a_r = [jnp.where(incl, amat[p][CHUNK:], 0.0) for p in ps]
        y = [xs[p][CHUNK:] + mm(a_r[p], jnp.concatenate([bd(u[p]), v_bd[p]], axis=0)) for p in ps]
        for p in ps:
            uv = jnp.concatenate([u[p], v[p]], axis=0)
            bkh = jnp.concatenate([bh_ref[p, sl, :], kh_ref[p, sl, :]], axis=0)
            upd = jnp.where(same_head, _dot_tn(uv, bkh), 0.0)
            y_ref[p, sl, :] = y[p]
            s_ref[p] = s[p] * pc_ref[p, c][0:1, :] + upd
        return carry

    lax.fori_loop(0, n_chunks, chunk, 0)

    ones = _head_ones(PAIR)
    for p in range(pairs):
        y = y_ref[p]
        mean = _head_sum(y, ones) * (1.0 / HEAD_SIZE)
        yc = y - mean
        var = _head_sum(yc * yc, ones) * (1.0 / HEAD_SIZE)
        yn = yc * lax.rsqrt(var + GN_EPS) * lg_ref[:, p * PAIR:(p + 1) * PAIR]
        yn = yn + lb_ref[:, p * PAIR:(p + 1) * PAIR] + bonus_ref[p]
        o_ref[:, p * PAIR:(p + 1) * PAIR] = (yn * gate_ref[p]).astype(o_ref.dtype)


def _wkv(prep_out, lnx_gain, lnx_bias, layer, batch, seq, tc, pairs_per_step):
    rt, at, kt, bt, kh, bh, vv, gate, bonus, pc = prep_out
    n_pairs = rt.shape[1]
    n_tc = seq // tc
    pb = pairs_per_step
    blk = pl.BlockSpec((None, pb, tc, PAIR), lambda b, p, c: (b, p, c, 0))
    pc_blk = pl.BlockSpec((None, pb, tc // CHUNK, SUBLANES, PAIR), lambda b, p, c: (b, p, c, 0, 0))
    row = pl.BlockSpec((None, 1, pb * PAIR), lambda b, p, c: (layer, 0, p))
    return pl.pallas_call(
        _wkv_kernel,
        grid=(batch, n_pairs // pb, n_tc),
        in_specs=[blk] * 9 + [pc_blk, row, row],
        out_specs=pl.BlockSpec((tc, pb * PAIR), lambda b, p, c: (b * n_tc + c, p)),
        out_shape=jax.ShapeDtypeStruct((batch * seq, n_pairs * PAIR), BF16),
        scratch_shapes=[pltpu.VMEM((pb, PAIR, PAIR), F32),
                        pltpu.VMEM((pb, tc, PAIR), F32)],
        compiler_params=_params(3),
        name="wkv7",
    )(rt, at, kt, bt, kh, bh, vv, gate, bonus, pc, lnx_gain, lnx_bias)


def _tile(n, target):
    t = min(n, target)
    while n % t:
        t //= 2
    return t


def kernel(x, c, ada_w, ada_b, norm_gain, ffn1_w_in, ffn1_w_out, ffn2_w_in, ffn2_w_out, mix_w_in,
           mix_w_in_vres, shift_mu, shift_mu_vres, pool_w, pool_scale, decay_w0, decay_w2, iclr_a0,
           iclr_a2, gate_g2, vres_v0, vres_v2, k_k, k_a, r_k, lnx_gain, lnx_bias, mix_w_out,
           final_gain):
    batch, seq, d = x.shape
    depth = ada_w.shape[0]
    m = batch * seq
    d_pool = pool_scale.shape[1]
    d_rwkv = decay_w0.shape[1]
    d_w, d_a, d_g = decay_w2.shape[1], iclr_a2.shape[1], gate_g2.shape[1]
    d_v = vres_v2.shape[1]
    lo_blk = d_w + d_a + d_g + d_v
    d_main = d_pool + 3 * d_rwkv
    assert d_pool == d_rwkv and lo_blk % LANES == 0
    assert d_rwkv % PAIR == 0 and seq % (2 * CHUNK) == 0
    lora_bounds = (d_w, d_w + d_a, d_w + d_a + d_g)

    tm_big = _tile(seq, 1024)
    tm_mid = _tile(seq, 512)
    tm_norm = _tile(seq, 512)
    tm_prep = 2 * CHUNK
    tc = _tile(seq, 256)
    pairs_per_step = _tile(d_rwkv // PAIR, 16)

    c_pad = jnp.pad(c, ((0, SUBLANES - batch), (0, 0)))
    mod = _ada(c_pad, ada_w, ada_b, _tile(d, 256), _tile(N_MOD * d, N_MOD * 1024))
    mod = mod.reshape(depth * SUBLANES * N_MOD, 1, d)
    gains = norm_gain.reshape(depth * 3, 1, d)
    pool_scale3 = pool_scale.reshape(depth, 1, d_pool)
    lnx_gain3 = lnx_gain.reshape(depth, 1, d_rwkv)
    lnx_bias3 = lnx_bias.reshape(depth, 1, d_rwkv)
    mix_w_in_t = jnp.swapaxes(mix_w_in, 1, 2)

    xf = x.reshape(m, d)
    vfirst = None
    for l in range(depth):
        def row_of(b, j, l=l):
            return (l * SUBLANES + b) * N_MOD + j

        h = _norm_mod(xf, gains, 3 * l, mod, row_of, 0, 1, seq, tm_norm)
        act = _mm_swiglu(h, ffn1_w_in, l, tm_big, _tile(ffn1_w_in.shape[2] // 2, 256))
        xf = _mm_resid([act], ffn1_w_out, l, xf, mod, row_of, 2, 0.5, seq, tm_mid, _tile(d, 512))

        h = _norm_mod(xf, gains, 3 * l + 1, mod, row_of, 3, 4, seq, tm_norm)
        proj = _mm_plain(h, mix_w_in_t, l, d_main, tm_big, _tile(d_main, 512), F32)
        if l == 0:
            w_tail_t = jnp.zeros((d_v, d), F32)
            mu_tail = jnp.zeros((d_v,), F32)
        else:
            w_tail_t = mix_w_in_vres[l - 1].T
            mu_tail = shift_mu_vres[l - 1]
        w_lora_t = jnp.concatenate([mix_w_in_t[l, d_main:], w_tail_t], axis=0)[None]
        lora_in = _mm_plain(h, w_lora_t, 0, lo_blk, tm_big, lo_blk, F32)
        mu = jnp.concatenate([shift_mu[l], mu_tail])

        pool_out = _pool(proj, pool_w, pool_scale3, l, batch, seq)

        def pad_rows(w, start):
            return jnp.pad(w, ((start, lo_blk - start - w.shape[0]), (0, 0)))

        rows = {"mu_r": mu[0:d_rwkv], "mu_k": mu[d_rwkv:2 * d_rwkv], "mu_v": mu[2 * d_rwkv:3 * d_rwkv],
                "mu_lo": mu[3 * d_rwkv:], "w0": decay_w0[l], "a0": iclr_a0[l], "k_k": k_k[l],
                "k_a": k_a[l], "r_k": r_k[l].reshape(-1)}
        mats = [pad_rows(decay_w2[l], 0), pad_rows(iclr_a2[l], d_w), pad_rows(gate_g2[l], d_w + d_a)]
        if l > 0:
            rows["v0"] = vres_v0[l - 1]
            mats.append(pad_rows(vres_v2[l - 1], d_w + d_a + d_g))
        mats = jnp.stack(mats)
        rows = {k_: v_.reshape(1, -1) for k_, v_ in rows.items()}
        prep_out = _prep(proj, lora_in, vfirst, rows, mats, lora_bounds, batch, seq, d_rwkv, lo_blk,
                         tm_prep)
        if l == 0:
            vfirst = prep_out[6]
        rwkv_out = _wkv(prep_out, lnx_gain3, lnx_bias3, l, batch, seq, tc, pairs_per_step)

        xf = _mm_resid([pool_out, rwkv_out], mix_w_out, l, xf, mod, row_of, 5, 1.0, seq, tm_big,
                       _tile(d, 512))

        h = _norm_mod(xf, gains, 3 * l + 2, mod, row_of, 6, 7, seq, tm_norm)
        act = _mm_swiglu(h, ffn2_w_in, l, tm_big, _tile(ffn2_w_in.shape[2] // 2, 256))
        xf = _mm_resid([act], ffn2_w_out, l, xf, mod, row_of, 8, 0.5, seq, tm_mid, _tile(d, 512))

    return _final_norm(xf, final_gain, tm_norm).reshape(batch, seq, d)
```

```python
import functools

import jax
import jax.numpy as jnp
from jax import lax
from jax.experimental import pallas as pl
from jax.experimental.pallas import tpu as pltpu

F32 = jnp.float32
BF16 = jnp.bfloat16

LANES = 128
SUBLANES = 8
HEAD_SIZE = 64
PAIR = 2 * HEAD_SIZE
CHUNK = 64
POOL_WINDOWS = (2, 4, 8, 16)
N_MOD = 9
RMS_EPS = 1e-6
GN_EPS = 64e-5
L2_EPS = 1e-12
DECAY_SCALE = 0.6065306597126334
VMEM_LIMIT_BYTES = 56 * 1024 * 1024


def _params(n_axes):
    return pltpu.CompilerParams(dimension_semantics=("arbitrary",) * n_axes,
                                vmem_limit_bytes=VMEM_LIMIT_BYTES)


def _dot(a, b):
    return jnp.dot(a, b, preferred_element_type=F32)


def _dot_nt(a, b):
    return lax.dot_general(a, b, (((1,), (1,)), ((), ())), preferred_element_type=F32)


def _dot_tn(a, b):
    return lax.dot_general(a, b, (((0,), (0,)), ((), ())), preferred_element_type=F32)


def _split2(x):
    hi = x.astype(BF16)
    lo = (x - hi.astype(F32)).astype(BF16)
    return hi, lo


def _split3(x):
    hi = x.astype(BF16)
    r1 = x - hi.astype(F32)
    mid = r1.astype(BF16)
    lo = (r1 - mid.astype(F32)).astype(BF16)
    return hi, mid, lo


def _head_ones(n, value=1.0):
    r = lax.broadcasted_iota(jnp.int32, (n, n), 0) // HEAD_SIZE
    c = lax.broadcasted_iota(jnp.int32, (n, n), 1) // HEAD_SIZE
    return jnp.where(r == c, value, 0.0).astype(BF16)


def _head_sum(x, ones):
    hi, lo = _split2(x)
    cols = []
    for j in range(x.shape[1] // LANES):
        sl = slice(j * LANES, (j + 1) * LANES)
        cols.append(_dot(hi[:, sl], ones) + _dot(lo[:, sl], ones))
    return cols[0] if len(cols) == 1 else jnp.concatenate(cols, axis=1)


def _ada_kernel(c_ref, w_ref, b_ref, o_ref):
    k = pl.program_id(2)
    c = c_ref[...]
    act = (c * jax.nn.sigmoid(c)).astype(BF16)
    part = _dot(act, w_ref[...].astype(BF16))

    @pl.when(k == 0)
    def _():
        o_ref[...] = part + b_ref[...]

    @pl.when(k > 0)
    def _():
        o_ref[...] += part


def _ada(c_pad, ada_w, ada_b, tk, tn):
    depth, d, n = ada_w.shape
    rows = c_pad.shape[0]
    return pl.pallas_call(
        _ada_kernel,
        grid=(depth, n // tn, d // tk),
        in_specs=[pl.BlockSpec((rows, tk), lambda l, j, k: (0, k)),
                  pl.BlockSpec((None, tk, tn), lambda l, j, k: (l, k, j)),
                  pl.BlockSpec((None, 1, tn), lambda l, j, k: (l, 0, j))],
        out_specs=pl.BlockSpec((None, rows, tn), lambda l, j, k: (l, 0, j)),
        out_shape=jax.ShapeDtypeStruct((depth, rows, n), F32),
        compiler_params=_params(3),
        name="ada_mod",
    )(c_pad, ada_w, ada_b.reshape(depth, 1, n))


def _norm_mod_kernel(x_ref, g_ref, sh_ref, sc_ref, o_ref):
    x = x_ref[...]
    y = x * lax.rsqrt(jnp.mean(x * x, axis=-1, keepdims=True) + RMS_EPS) * g_ref[...]
    o_ref[...] = (y * (1 + sc_ref[...]) + sh_ref[...]).astype(o_ref.dtype)


def _norm_mod(x, gains, gain_row, mod, row_of, j_shift, j_scale, seq, tm):
    m, d = x.shape
    per_seq = seq // tm
    return pl.pallas_call(
        _norm_mod_kernel,
        grid=(m // tm,),
        in_specs=[pl.BlockSpec((tm, d), lambda i: (i, 0)),
                  pl.BlockSpec((None, 1, d), lambda i: (gain_row, 0, 0)),
                  pl.BlockSpec((None, 1, d), lambda i: (row_of(i // per_seq, j_shift), 0, 0)),
                  pl.BlockSpec((None, 1, d), lambda i: (row_of(i // per_seq, j_scale), 0, 0))],
        out_specs=pl.BlockSpec((tm, d), lambda i: (i, 0)),
        out_shape=jax.ShapeDtypeStruct((m, d), BF16),
        compiler_params=_params(1),
        name="norm_mod",
    )(x, gains, mod, mod)


def _final_norm_kernel(x_ref, g_ref, o_ref):
    x = x_ref[...]
    o_ref[...] = x * lax.rsqrt(jnp.mean(x * x, axis=-1, keepdims=True) + RMS_EPS) * g_ref[...]


def _final_norm(x, gain, tm):
    m, d = x.shape
    return pl.pallas_call(
        _final_norm_kernel,
        grid=(m // tm,),
        in_specs=[pl.BlockSpec((tm, d), lambda i: (i, 0)),
                  pl.BlockSpec((1, d), lambda i: (0, 0))],
        out_specs=pl.BlockSpec((tm, d), lambda i: (i, 0)),
        out_shape=jax.ShapeDtypeStruct((m, d), F32),
        compiler_params=_params(1),
        name="final_norm",
    )(x, gain.reshape(1, d))


def _mm_plain_kernel(a_ref, wt_ref, o_ref, wb_ref):
    @pl.when(pl.program_id(1) == 0)
    def _():
        wb_ref[...] = wt_ref[...].T.astype(BF16)
    o_ref[...] = _dot(a_ref[...], wb_ref[...]).astype(o_ref.dtype)


def _mm_plain(a, wt, layer, n, tm, tn, out_dtype):
    m, k = a.shape
    return pl.pallas_call(
        _mm_plain_kernel,
        grid=(n // tn, m // tm),
        in_specs=[pl.BlockSpec((tm, k), lambda j, i: (i, 0)),
                  pl.BlockSpec((None, tn, k), lambda j, i: (layer, j, 0))],
        out_specs=pl.BlockSpec((tm, tn), lambda j, i: (i, j)),
        out_shape=jax.ShapeDtypeStruct((m, n), out_dtype),
        scratch_shapes=[pltpu.VMEM((k, tn), BF16)],
        compiler_params=_params(2),
        name="mm_plain",
    )(a, wt)


class _WeightTiles:
    def __init__(self, w_hbm, layer, wb_refs, stage_ref, sem, n_inner):
        self.w_hbm, self.layer = w_hbm, layer
        self.wb_refs, self.stage_ref, self.sem = wb_refs, stage_ref, sem
        k, self.tn = wb_refs[0].shape
        self.rows = k // n_inner
        self.n_inner = n_inner

    def _row_slice(self, chunk):
        start = chunk * self.rows
        if not isinstance(chunk, int):
            start = pl.multiple_of(start, self.rows)
        return pl.ds(start, self.rows)

    def _copy(self, tile, chunk, slot):
        src = self.w_hbm.at[self.layer, self._row_slice(chunk), pl.ds(tile * self.tn, self.tn)]
        return pltpu.make_async_copy(src, self.stage_ref.at[slot], self.sem.at[slot])

    def _land(self, tile, chunk, slot, dst_ref):
        self._copy(tile, chunk, slot).wait()
        dst_ref[self._row_slice(chunk), :] = self.stage_ref[slot].astype(BF16)

    def _load_first(self):
        self._copy(0, 0, 0).start()
        for chunk in range(self.n_inner):
            if chunk + 1 < self.n_inner:
                self._copy(0, chunk + 1, (chunk + 1) % 2).start()
            self._land(0, chunk, chunk % 2, self.wb_refs[0])

    def run(self, step):
        j, i = pl.program_id(0), pl.program_id(1)
        nxt = jnp.minimum(j + 1, pl.num_programs(0) - 1)

        @pl.when((j == 0) & (i == 0))
        def _():
            self._load_first()

        self._copy(nxt, i, 0).start(priority=1)
        for parity in range(2):
            @pl.when(j % 2 == parity)
            def _(parity=parity):
                step(self.wb_refs[parity])
        for parity in range(2):
            @pl.when(j % 2 == parity)
            def _(parity=parity):
                self._land(nxt, i, 0, self.wb_refs[1 - parity])


def _weight_tile_scratch(k, tn, n_inner):
    assert k % n_inner == 0 and (k // n_inner) % (2 * SUBLANES) == 0
    return [pltpu.VMEM((k, tn), BF16), pltpu.VMEM((k, tn), BF16),
            pltpu.VMEM((2, k // n_inner, tn), F32), pltpu.SemaphoreType.DMA((2,))]


def _mm_swiglu_kernel(a_ref, wg_ref, wu_ref, o_ref, wgb_ref, wub_ref):
    @pl.when(pl.program_id(1) == 0)
    def _():
        wgb_ref[...] = wg_ref[...].astype(BF16)
        wub_ref[...] = wu_ref[...].astype(BF16)
    a = a_ref[...]
    g = _dot(a, wgb_ref[...])
    u = _dot(a, wub_ref[...])
    o_ref[...] = (g * jax.nn.sigmoid(g) * u).astype(o_ref.dtype)


def _mm_swiglu(a, w_in, layer, tm, tn):
    m, k = a.shape
    f = w_in.shape[2] // 2
    nb = f // tn
    return pl.pallas_call(
        _mm_swiglu_kernel,
        grid=(nb, m // tm),
        in_specs=[pl.BlockSpec((tm, k), lambda j, i: (i, 0)),
                  pl.BlockSpec((None, k, tn), lambda j, i: (layer, 0, j)),
                  pl.BlockSpec((None, k, tn), lambda j, i: (layer, 0, j + nb))],
        out_specs=pl.BlockSpec((tm, tn), lambda j, i: (i, j)),
        out_shape=jax.ShapeDtypeStruct((m, f), BF16),
        scratch_shapes=[pltpu.VMEM((k, tn), BF16), pltpu.VMEM((k, tn), BF16)],
        compiler_params=_params(2),
        name="mm_swiglu",
    )(a, w_in, w_in)


def _mm_resid_kernel(*refs, n_a, coef, layer, n_inner):
    a_refs = refs[:n_a]
    w_hbm, x_ref, g_ref, o_ref, wb0_ref, wb1_ref, stage_ref, sem = refs[n_a:]

    def step(wb_ref):
        acc = None
        k0 = 0
        for a_ref in a_refs:
            kw = a_ref.shape[1]
            part = _dot(a_ref[...], wb_ref[k0:k0 + kw, :])
            acc = part if acc is None else acc + part
            k0 += kw
        o_ref[...] = x_ref[...] + (coef * g_ref[...]) * acc

    _WeightTiles(w_hbm, layer, (wb0_ref, wb1_ref), stage_ref, sem, n_inner).run(step)


def _mm_resid(a_parts, w, layer, x, mod, row_of, j_gate, coef, seq, tm, tn):
    m = x.shape[0]
    k, n = w.shape[1], w.shape[2]
    per_seq = seq // tm
    n_inner = m // tm
    return pl.pallas_call(
        functools.partial(_mm_resid_kernel, n_a=len(a_parts), coef=coef, layer=layer,
                          n_inner=n_inner),
        grid=(n // tn, n_inner),
        in_specs=[pl.BlockSpec((tm, a.shape[1]), lambda j, i: (i, 0)) for a in a_parts] + [
                  pl.BlockSpec(memory_space=pl.ANY),
                  pl.BlockSpec((tm, tn), lambda j, i: (i, j)),
                  pl.BlockSpec((None, 1, tn), lambda j, i: (row_of(i // per_seq, j_gate), 0, j))],
        out_specs=pl.BlockSpec((tm, tn), lambda j, i: (i, j)),
        out_shape=jax.ShapeDtypeStruct((m, n), F32),
        scratch_shapes=_weight_tile_scratch(k, tn, n_inner),
        compiler_params=_params(2),
        name="mm_resid",
    )(*a_parts, w, x, mod)


def _shift_rows(x, k):
    assert k <= SUBLANES
    rolled = pltpu.roll(x, k, axis=0)
    row = lax.broadcasted_iota(jnp.int32, (SUBLANES, x.shape[1]), 0)
    top = jnp.where(row < k, 0.0, rolled[:SUBLANES])
    return jnp.concatenate([top, rolled[SUBLANES:]], axis=0)


def _pool_kernel(p_ref, w_ref, s_ref, o_ref):
    g = pl.program_id(0)
    for gi, win in enumerate(POOL_WINDOWS):
        @pl.when(g == gi)
        def _(win=win):
            x = p_ref[...]
            acc = x
            span = 1
            while span < win:
                acc = acc + _shift_rows(acc, span)
                span *= 2
            row = lax.broadcasted_iota(jnp.int32, (x.shape[0], 1), 0)
            count = jnp.minimum(row + 1, win).astype(F32)
            pooled = (acc / count - x).astype(BF16)
            o_ref[...] = (_dot(pooled, w_ref[...].astype(BF16)) * s_ref[...]).astype(o_ref.dtype)


def _pool(proj, pool_w, pool_scale, layer, batch, seq):
    n_groups, cg = pool_w.shape[1], pool_w.shape[2]
    assert n_groups == len(POOL_WINDOWS)
    m = proj.shape[0]
    return pl.pallas_call(
        _pool_kernel,
        grid=(n_groups, batch),
        in_specs=[pl.BlockSpec((seq, cg), lambda g, b: (b, g)),
                  pl.BlockSpec((None, None, cg, cg), lambda g, b: (layer, g, 0, 0)),
                  pl.BlockSpec((None, 1, cg), lambda g, b: (layer, 0, g))],
        out_specs=pl.BlockSpec((seq, cg), lambda g, b: (b, g)),
        out_shape=jax.ShapeDtypeStruct((m, n_groups * cg), BF16),
        compiler_params=_params(2),
        name="pool_mix",
    )(proj, pool_w, pool_scale)


def _prep_kernel(*refs, has_vres, lora_bounds, n_pairs, tiles_per_seq):
    if has_vres:
        (r_ref, k_ref, v_ref, lo_ref, rp_ref, kp_ref, vp_ref, lop_ref,
         mur_ref, muk_ref, muv_ref, mulo_ref, w0_ref, a0_ref, kk_ref, ka_ref, rk_ref,
         lw_ref, v0_ref, vf_ref,
         rt_ref, at_ref, kt_ref, bt_ref, kh_ref, bh_ref, vv_ref, gate_ref, bonus_ref, pc_ref,
         lwb_ref) = refs
    else:
        (r_ref, k_ref, v_ref, lo_ref, rp_ref, kp_ref, vp_ref, lop_ref,
         mur_ref, muk_ref, muv_ref, mulo_ref, w0_ref, a0_ref, kk_ref, ka_ref, rk_ref,
         lw_ref,
         rt_ref, at_ref, kt_ref, bt_ref, kh_ref, bh_ref, vv_ref, gate_ref, bonus_ref, pc_ref,
         lwb_ref) = refs

    @pl.when(pl.program_id(0) == 0)
    def _():
        lwb_ref[...] = lw_ref[...].astype(BF16)
    first = (pl.program_id(0) % tiles_per_seq) == 0
    tm = r_ref.shape[0]

    def shift_mix(cur_ref, prev_ref, mu_ref, cols):
        z = cur_ref[:, cols]
        rolled = pltpu.roll(z, 1, axis=0)
        row = lax.broadcasted_iota(jnp.int32, (SUBLANES, z.shape[1]), 0)
        prev_last = jnp.where(first, 0.0, prev_ref[SUBLANES - 1:SUBLANES, cols])
        top = jnp.where(row == 0, prev_last, rolled[:SUBLANES])
        zp = jnp.concatenate([top, rolled[SUBLANES:]], axis=0)
        return z + (zp - z) * mu_ref[:, cols]

    lo = shift_mix(lo_ref, lop_ref, mulo_ref, slice(None))
    e_w, e_a, e_g = lora_bounds
    lane = lax.broadcasted_iota(jnp.int32, lo.shape, 1)
    act = jnp.where(lane < e_w, jnp.tanh(lo),
                    jnp.where((lane >= e_a) & (lane < e_g), jax.nn.sigmoid(lo), lo)).astype(BF16)

    ri = lax.broadcasted_iota(jnp.int32, (tm, tm), 0)
    ci = lax.broadcasted_iota(jnp.int32, (tm, tm), 1)
    same = (ri // CHUNK) == (ci // CHUNK)
    sel = jnp.concatenate([(same & (ci <= ri)).astype(BF16),
                           (same & (ci > ri)).astype(BF16),
                           same.astype(BF16)], axis=0)
    ones = _head_ones(PAIR)

    lw_all = -DECAY_SCALE * jax.nn.sigmoid(w0_ref[...] + _dot(act, lwb_ref[0]))
    iclr_all = jax.nn.sigmoid(a0_ref[...] + _dot(act, lwb_ref[1]))
    gate_all = _dot(act, lwb_ref[2])
    if has_vres:
        mix_all = jax.nn.sigmoid(v0_ref[...] + _dot(act, lwb_ref[3]))

    for p in range(n_pairs):
        cols = slice(p * PAIR, (p + 1) * PAIR)
        r = shift_mix(r_ref, rp_ref, mur_ref, cols)
        k = shift_mix(k_ref, kp_ref, muk_ref, cols)
        v = shift_mix(v_ref, vp_ref, muv_ref, cols)
        lw = lw_all[:, cols]
        iclr = iclr_all[:, cols]
        gate_ref[p] = gate_all[:, cols]
        if has_vres:
            v = v + (vf_ref[p] - v) * mix_all[:, cols]
        vv_ref[p] = v

        kk = k * kk_ref[:, cols]
        kk = kk * lax.rsqrt(jnp.maximum(_head_sum(kk * kk, ones), L2_EPS * L2_EPS))
        km = k * (1 + (iclr - 1) * ka_ref[:, cols])
        bonus_ref[p] = _head_sum(r * km * rk_ref[:, cols], ones) * v
        a = -kk
        b = kk * iclr

        h3, m3, l3 = _split3(lw)
        sums = _dot(sel, h3) + _dot(sel, m3) + _dot(sel, l3)
        cum, rest, tot = sums[:tm], sums[tm:2 * tm], sums[2 * tm:]
        p_inv = jnp.exp(-cum)
        p_rest = jnp.exp(rest)
        rt_ref[p] = (r * jnp.exp(cum)).astype(BF16)
        at_ref[p] = (a * jnp.exp(cum - lw)).astype(BF16)
        kt_ref[p] = (km * p_inv).astype(BF16)
        bt_ref[p] = (b * p_inv).astype(BF16)
        kh_ref[p] = (km * p_rest).astype(BF16)
        bh_ref[p] = (b * p_rest).astype(BF16)
        p_tot = jnp.exp(tot)
        for c in range(tm // CHUNK):
            pc_ref[p, c] = p_tot[c * CHUNK:c * CHUNK + SUBLANES]


def _prep(proj, lora_in, vfirst, rows, mats, lora_bounds, batch, seq, d_rwkv, lo_blk, tm):
    m = proj.shape[0]
    has_vres = vfirst is not None
    n_pairs = d_rwkv // PAIR
    tiles_per_seq = seq // tm
    sub = tm // SUBLANES

    def cur(width, col):
        return pl.BlockSpec((tm, width), lambda i: (i, col))

    def prev(width, col):
        return pl.BlockSpec((SUBLANES, width), lambda i: (jnp.maximum(i * sub - 1, 0), col))

    def row(width):
        return pl.BlockSpec((1, width), lambda i: (0, 0))

    in_specs = [cur(d_rwkv, 1), cur(d_rwkv, 2), cur(d_rwkv, 3), cur(lo_blk, 0),
                prev(d_rwkv, 1), prev(d_rwkv, 2), prev(d_rwkv, 3), prev(lo_blk, 0),
                row(d_rwkv), row(d_rwkv), row(d_rwkv), row(lo_blk)]
    args = [proj] * 3 + [lora_in] + [proj] * 3 + [lora_in]
    args += [rows["mu_r"], rows["mu_k"], rows["mu_v"], rows["mu_lo"]]
    for name in ("w0", "a0", "k_k", "k_a", "r_k"):
        in_specs.append(row(d_rwkv))
        args.append(rows[name])
    in_specs.append(pl.BlockSpec(mats.shape, lambda i: (0, 0, 0), pipeline_mode=pl.Buffered(1)))
    args.append(mats)
    if has_vres:
        in_specs += [row(d_rwkv),
                     pl.BlockSpec((None, n_pairs, tm, PAIR),
                                  lambda i: (i // tiles_per_seq, 0, i % tiles_per_seq, 0))]
        args += [rows["v0"], vfirst]

    pair_spec = pl.BlockSpec((None, n_pairs, tm, PAIR),
                             lambda i: (i // tiles_per_seq, 0, i % tiles_per_seq, 0))
    def pair_shape(dtype):
        return jax.ShapeDtypeStruct((batch, n_pairs, seq, PAIR), dtype)
    pair_shapes = [pair_shape(BF16)] * 6 + [pair_shape(F32)] * 3
    cpt = tm // CHUNK
    pc_spec = pl.BlockSpec((None, n_pairs, cpt, SUBLANES, PAIR),
                           lambda i: (i // tiles_per_seq, 0, i % tiles_per_seq, 0, 0))
    pc_shape = jax.ShapeDtypeStruct((batch, n_pairs, seq // CHUNK, SUBLANES, PAIR), F32)
    return pl.pallas_call(
        functools.partial(_prep_kernel, has_vres=has_vres, lora_bounds=lora_bounds,
                          n_pairs=n_pairs, tiles_per_seq=tiles_per_seq),
        grid=(m // tm,),
        in_specs=in_specs,
        out_specs=[pair_spec] * 9 + [pc_spec],
        out_shape=pair_shapes + [pc_shape],
        scratch_shapes=[pltpu.VMEM(mats.shape, BF16)],
        compiler_params=_params(1),
        name="rwkv_prep",
    )(*args)


def _wkv_kernel(rt_ref, at_ref, kt_ref, bt_ref, kh_ref, bh_ref, v_ref, gate_ref, bonus_ref, pc_ref,
                lg_ref, lb_ref, o_ref, s_ref, y_ref):
    @pl.when(pl.program_id(2) == 0)
    def _():
        s_ref[...] = jnp.zeros_like(s_ref)

    pairs = rt_ref.shape[0]
    n_chunks = rt_ref.shape[1] // CHUNK
    head0 = lax.broadcasted_iota(jnp.int32, (CHUNK, PAIR), 1) < HEAD_SIZE
    row = lax.broadcasted_iota(jnp.int32, (CHUNK, 2 * PAIR), 0)
    col = lax.broadcasted_iota(jnp.int32, (CHUNK, 2 * PAIR), 1) % HEAD_SIZE
    strict = (row > col)[:, :PAIR]
    incl = row >= col
    eye = (row == col)[:, :PAIR].astype(F32)
    vi = lax.broadcasted_iota(jnp.int32, (PAIR, PAIR), 0) // HEAD_SIZE
    ki = lax.broadcasted_iota(jnp.int32, (PAIR, PAIR), 1) // HEAD_SIZE
    same_head = vi == ki

    def bd(x):
        x = x.astype(BF16)
        return jnp.concatenate([jnp.where(head0, x, 0.0), jnp.where(head0, 0.0, x)], axis=0)

    def mm(a, b_bd):
        return _dot(a.astype(BF16), b_bd)

    def chunk(c, carry):
        sl = pl.ds(pl.multiple_of(c * CHUNK, CHUNK), CHUNK)
        ps = range(pairs)
        ar = [jnp.concatenate([at_ref[p, sl, :], rt_ref[p, sl, :]], axis=0) for p in ps]
        bk = [jnp.concatenate([bd(bt_ref[p, sl, :]), bd(kt_ref[p, sl, :])], axis=0) for p in ps]
        v = [v_ref[p, sl, :].astype(BF16) for p in ps]
        v_bd = [bd(v[p]) for p in ps]
        s = [s_ref[p] for p in ps]
        amat = [_dot_nt(ar[p], bk[p]) for p in ps]
        xs = [_dot_nt(ar[p], s[p].astype(BF16)) for p in ps]
        a_ak = [jnp.where(strict, amat[p][:CHUNK, PAIR:], 0.0) for p in ps]
        rhs = [xs[p][:CHUNK] + mm(a_ak[p], v_bd[p]) for p in ps]

        pw = [jnp.where(strict, amat[p][:CHUNK, :PAIR], 0.0) for p in ps]
        inv = [eye + pw[p] for p in ps]
        pw = [mm(pw[p], bd(pw[p])) for p in ps]
        span = 2
        while 2 * span < CHUNK:
            both = [mm(jnp.concatenate([pw[p], inv[p]], axis=0), bd(pw[p])) for p in ps]
            inv = [inv[p] + both[p][CHUNK:] for p in ps]
            pw = [both[p][:CHUNK] for p in ps]
            span *= 2
        inv = [inv[p] + mm(inv[p], bd(pw[p])) for p in ps]

        u = [mm(inv[p], bd(rhs[p])).astype(BF16) for p in ps]
        a_r = [jnp.where(incl, amat[p][CHUNK:], 0.0) for p in ps]
        y = [xs[p][CHUNK:] + mm(a_r[p], jnp.concatenate([bd(u[p]), v_bd[p]], axis=0)) for p in ps]
        for p in ps:
            uv = jnp.concatenate([u[p], v[p]], axis=0)
            bkh = jnp.concatenate([bh_ref[p, sl, :], kh_ref[p, sl, :]], axis=0)
            upd = jnp.where(same_head, _dot_tn(uv, bkh), 0.0)
            y_ref[p, sl, :] = y[p]
            s_ref[p] = s[p] * pc_ref[p, c][0:1, :] + upd
        return carry

    lax.fori_loop(0, n_chunks, chunk, 0)

    averager = _head_ones(PAIR, 1.0 / HEAD_SIZE)
    for p in range(pairs):
        y = y_ref[p]
        mean = _head_sum(y, averager)
        yc = y - mean
        var = _head_sum(yc * yc, averager)
        yn = yc * lax.rsqrt(var + GN_EPS) * lg_ref[:, p * PAIR:(p + 1) * PAIR]
        yn = yn + lb_ref[:, p * PAIR:(p + 1) * PAIR] + bonus_ref[p]
        o_ref[:, p * PAIR:(p + 1) * PAIR] = (yn * gate_ref[p]).astype(o_ref.dtype)


def _wkv(prep_out, lnx_gain, lnx_bias, layer, batch, seq, tc, pairs_per_step):
    rt, at, kt, bt, kh, bh, vv, gate, bonus, pc = prep_out
    n_pairs = rt.shape[1]
    n_tc = seq // tc
    pb = pairs_per_step
    blk = pl.BlockSpec((None, pb, tc, PAIR), lambda b, p, c: (b, p, c, 0))
    pc_blk = pl.BlockSpec((None, pb, tc // CHUNK, SUBLANES, PAIR), lambda b, p, c: (b, p, c, 0, 0))
    row = pl.BlockSpec((None, 1, pb * PAIR), lambda b, p, c: (layer, 0, p))
    return pl.pallas_call(
        _wkv_kernel,
        grid=(batch, n_pairs // pb, n_tc),
        in_specs=[blk] * 9 + [pc_blk, row, row],
        out_specs=pl.BlockSpec((tc, pb * PAIR), lambda b, p, c: (b * n_tc + c, p)),
        out_shape=jax.ShapeDtypeStruct((batch * seq, n_pairs * PAIR), BF16),
        scratch_shapes=[pltpu.VMEM((pb, PAIR, PAIR), F32),
                        pltpu.VMEM((pb, tc, PAIR), F32)],
        compiler_params=_params(3),
        name="wkv7",
    )(rt, at, kt, bt, kh, bh, vv, gate, bonus, pc, lnx_gain, lnx_bias)


def _tile(n, target):
    t = min(n, target)
    while n % t:
        t //= 2
    return t


def kernel(x, c, ada_w, ada_b, norm_gain, ffn1_w_in, ffn1_w_out, ffn2_w_in, ffn2_w_out, mix_w_in,
           mix_w_in_vres, shift_mu, shift_mu_vres, pool_w, pool_scale, decay_w0, decay_w2, iclr_a0,
           iclr_a2, gate_g2, vres_v0, vres_v2, k_k, k_a, r_k, lnx_gain, lnx_bias, mix_w_out,
           final_gain):
    batch, seq, d = x.shape
    depth = ada_w.shape[0]
    m = batch * seq
    d_pool = pool_scale.shape[1]
    d_rwkv = decay_w0.shape[1]
    d_w, d_a, d_g = decay_w2.shape[1], iclr_a2.shape[1], gate_g2.shape[1]
    d_v = vres_v2.shape[1]
    lo_blk = d_w + d_a + d_g + d_v
    d_main = d_pool + 3 * d_rwkv
    assert d_pool == d_rwkv and lo_blk % LANES == 0
    assert d_rwkv % PAIR == 0 and seq % (2 * CHUNK) == 0
    lora_bounds = (d_w, d_w + d_a, d_w + d_a + d_g)

    tm_big = _tile(seq, 1024)
    tm_mid = _tile(seq, 512)
    tm_norm = _tile(seq, 512)
    tm_prep = 2 * CHUNK
    tc = _tile(seq, 256)
    pairs_per_step = _tile(d_rwkv // PAIR, 16)

    c_pad = jnp.pad(c, ((0, SUBLANES - batch), (0, 0)))
    mod = _ada(c_pad, ada_w, ada_b, _tile(d, 512), _tile(N_MOD * d, N_MOD * 1024))
    mod = mod.reshape(depth * SUBLANES * N_MOD, 1, d)
    gains = norm_gain.reshape(depth * 3, 1, d)
    pool_scale3 = pool_scale.reshape(depth, 1, d_pool)
    lnx_gain3 = lnx_gain.reshape(depth, 1, d_rwkv)
    lnx_bias3 = lnx_bias.reshape(depth, 1, d_rwkv)
    mix_w_in_t = jnp.swapaxes(mix_w_in, 1, 2)

    xf = x.reshape(m, d)
    vfirst = None
    for l in range(depth):
        def row_of(b, j, l=l):
            return (l * SUBLANES + b) * N_MOD + j

        h = _norm_mod(xf, gains, 3 * l, mod, row_of, 0, 1, seq, tm_norm)
        act = _mm_swiglu(h, ffn1_w_in, l, tm_big, _tile(ffn1_w_in.shape[2] // 2, 256))
        xf = _mm_resid([act], ffn1_w_out, l, xf, mod, row_of, 2, 0.5, seq, tm_mid, _tile(d, 512))

        h = _norm_mod(xf, gains, 3 * l + 1, mod, row_of, 3, 4, seq, tm_norm)
        proj = _mm_plain(h, mix_w_in_t, l, d_main, tm_big, _tile(d_main, 512), F32)
        if l == 0:
            w_tail_t = jnp.zeros((d_v, d), F32)
            mu_tail = jnp.zeros((d_v,), F32)
        else:
            w_tail_t = mix_w_in_vres[l - 1].T
            mu_tail = shift_mu_vres[l - 1]
        w_lora_t = jnp.concatenate([mix_w_in_t[l, d_main:], w_tail_t], axis=0)[None]
        lora_in = _mm_plain(h, w_lora_t, 0, lo_blk, tm_big, lo_blk, F32)
        mu = jnp.concatenate([shift_mu[l], mu_tail])

        pool_out = _pool(proj, pool_w, pool_scale3, l, batch, seq)

        def pad_rows(w, start):
            return jnp.pad(w, ((start, lo_blk - start - w.shape[0]), (0, 0)))

        rows = {"mu_r": mu[0:d_rwkv], "mu_k": mu[d_rwkv:2 * d_rwkv], "mu_v": mu[2 * d_rwkv:3 * d_rwkv],
                "mu_lo": mu[3 * d_rwkv:], "w0": decay_w0[l], "a0": iclr_a0[l], "k_k": k_k[l],
                "k_a": k_a[l], "r_k": r_k[l].reshape(-1)}
        mats = [pad_rows(decay_w2[l], 0), pad_rows(iclr_a2[l], d_w), pad_rows(gate_g2[l], d_w + d_a)]
        if l > 0:
            rows["v0"] = vres_v0[l - 1]
            mats.append(pad_rows(vres_v2[l - 1], d_w + d_a + d_g))
        mats = jnp.stack(mats)
        rows = {k_: v_.reshape(1, -1) for k_, v_ in rows.items()}
        prep_out = _prep(proj, lora_in, vfirst, rows, mats, lora_bounds, batch, seq, d_rwkv, lo_blk,
                         tm_prep)
        if l == 0:
            vfirst = prep_out[6]
        rwkv_out = _wkv(prep_out, lnx_gain3, lnx_bias3, l, batch, seq, tc, pairs_per_step)

        xf = _mm_resid([pool_out, rwkv_out], mix_w_out, l, xf, mod, row_of, 5, 1.0, seq, tm_big,
                       _tile(d, 512))

        h = _norm_mod(xf, gains, 3 * l + 2, mod, row_of, 6, 7, seq, tm_norm)
        act = _mm_swiglu(h, ffn2_w_in, l, tm_big, _tile(ffn2_w_in.shape[2] // 2, 256))
        xf = _mm_resid([act], ffn2_w_out, l, xf, mod, row_of, 8, 0.5, seq, tm_mid, _tile(d, 512))

    return _final_norm(xf, final_gain, tm_mid).reshape(batch, seq, d)
```

```python
import functools

import jax
import jax.numpy as jnp
from jax import lax
from jax.experimental import pallas as pl
from jax.experimental.pallas import tpu as pltpu

F32 = jnp.float32
BF16 = jnp.bfloat16

LANES = 128
SUBLANES = 8
HEAD_SIZE = 64
PAIR = 2 * HEAD_SIZE
CHUNK = 64
POOL_WINDOWS = (2, 4, 8, 16)
N_MOD = 9
RMS_EPS = 1e-6
GN_EPS = 64e-5
L2_EPS = 1e-12
DECAY_SCALE = 0.6065306597126334
ADA_SIDE_TK = 256
VMEM_LIMIT_BYTES = 56 * 1024 * 1024


def _params(n_axes):
    return pltpu.CompilerParams(dimension_semantics=("arbitrary",) * n_axes,
                                vmem_limit_bytes=VMEM_LIMIT_BYTES)


def _dot(a, b):
    return jnp.dot(a, b, preferred_element_type=F32)


def _dot_nt(a, b):
    return lax.dot_general(a, b, (((1,), (1,)), ((), ())), preferred_element_type=F32)


def _dot_tn(a, b):
    return lax.dot_general(a, b, (((0,), (0,)), ((), ())), preferred_element_type=F32)


def _split2(x):
    hi = x.astype(BF16)
    lo = (x - hi.astype(F32)).astype(BF16)
    return hi, lo


def _split3(x):
    hi = x.astype(BF16)
    r1 = x - hi.astype(F32)
    mid = r1.astype(BF16)
    lo = (r1 - mid.astype(F32)).astype(BF16)
    return hi, mid, lo


def _head_ones(n, value=1.0):
    r = lax.broadcasted_iota(jnp.int32, (n, n), 0) // HEAD_SIZE
    c = lax.broadcasted_iota(jnp.int32, (n, n), 1) // HEAD_SIZE
    return jnp.where(r == c, value, 0.0).astype(BF16)


def _head_sum(x, ones):
    hi, lo = _split2(x)
    cols = []
    for j in range(x.shape[1] // LANES):
        sl = slice(j * LANES, (j + 1) * LANES)
        cols.append(_dot(hi[:, sl], ones) + _dot(lo[:, sl], ones))
    return cols[0] if len(cols) == 1 else jnp.concatenate(cols, axis=1)


def _ada_kernel(c_ref, w_ref, b_ref, o_ref):
    k = pl.program_id(2)
    c = c_ref[...]
    act = (c * jax.nn.sigmoid(c)).astype(BF16)
    part = _dot(act, w_ref[...].astype(BF16))

    @pl.when(k == 0)
    def _():
        o_ref[...] = part + b_ref[...]

    @pl.when(k > 0)
    def _():
        o_ref[...] += part


def _ada(c_pad, ada_w, ada_b, layer, tk, tn):
    _, d, n = ada_w.shape
    depth = 1
    rows = c_pad.shape[0]
    return pl.pallas_call(
        _ada_kernel,
        grid=(depth, n // tn, d // tk),
        in_specs=[pl.BlockSpec((rows, tk), lambda l, j, k: (0, k)),
                  pl.BlockSpec((None, tk, tn), lambda l, j, k: (layer, k, j)),
                  pl.BlockSpec((None, 1, tn), lambda l, j, k: (layer, 0, j))],
        out_specs=pl.BlockSpec((None, rows, tn), lambda l, j, k: (l, 0, j)),
        out_shape=jax.ShapeDtypeStruct((depth, rows, n), F32),
        compiler_params=_params(3),
        name="ada_mod",
    )(c_pad, ada_w, ada_b.reshape(-1, 1, n))


def _norm_mod_kernel(x_ref, g_ref, sh_ref, sc_ref, o_ref):
    x = x_ref[...]
    y = x * lax.rsqrt(jnp.mean(x * x, axis=-1, keepdims=True) + RMS_EPS) * g_ref[...]
    o_ref[...] = (y * (1 + sc_ref[...]) + sh_ref[...]).astype(o_ref.dtype)


def _norm_mod(x, gains, gain_row, mod, row_of, j_shift, j_scale, seq, tm):
    m, d = x.shape
    per_seq = seq // tm
    return pl.pallas_call(
        _norm_mod_kernel,
        grid=(m // tm,),
        in_specs=[pl.BlockSpec((tm, d), lambda i: (i, 0)),
                  pl.BlockSpec((None, 1, d), lambda i: (gain_row, 0, 0)),
                  pl.BlockSpec((None, 1, d), lambda i: (row_of(i // per_seq, j_shift), 0, 0)),
                  pl.BlockSpec((None, 1, d), lambda i: (row_of(i // per_seq, j_scale), 0, 0))],
        out_specs=pl.BlockSpec((tm, d), lambda i: (i, 0)),
        out_shape=jax.ShapeDtypeStruct((m, d), BF16),
        compiler_params=_params(1),
        name="norm_mod",
    )(x, gains, mod, mod)


def _final_norm_kernel(x_ref, g_ref, o_ref):
    x = x_ref[...]
    o_ref[...] = x * lax.rsqrt(jnp.mean(x * x, axis=-1, keepdims=True) + RMS_EPS) * g_ref[...]


def _final_norm(x, gain, tm):
    m, d = x.shape
    return pl.pallas_call(
        _final_norm_kernel,
        grid=(m // tm,),
        in_specs=[pl.BlockSpec((tm, d), lambda i: (i, 0)),
                  pl.BlockSpec((1, d), lambda i: (0, 0))],
        out_specs=pl.BlockSpec((tm, d), lambda i: (i, 0)),
        out_shape=jax.ShapeDtypeStruct((m, d), F32),
        compiler_params=_params(1),
        name="final_norm",
    )(x, gain.reshape(1, d))


def _mm_plain_kernel(a_ref, wt_ref, o_ref, wb_ref):
    @pl.when(pl.program_id(1) == 0)
    def _():
        wb_ref[...] = wt_ref[...].T.astype(BF16)
    o_ref[...] = _dot(a_ref[...], wb_ref[...]).astype(o_ref.dtype)


def _mm_plain(a, wt, layer, n, tm, tn, out_dtype):
    m, k = a.shape
    return pl.pallas_call(
        _mm_plain_kernel,
        grid=(n // tn, m // tm),
        in_specs=[pl.BlockSpec((tm, k), lambda j, i: (i, 0)),
                  pl.BlockSpec((None, tn, k), lambda j, i: (layer, j, 0))],
        out_specs=pl.BlockSpec((tm, tn), lambda j, i: (i, j)),
        out_shape=jax.ShapeDtypeStruct((m, n), out_dtype),
        scratch_shapes=[pltpu.VMEM((k, tn), BF16)],
        compiler_params=_params(2),
        name="mm_plain",
    )(a, wt)


class _WeightTiles:
    def __init__(self, w_hbm, layer, wb_refs, stage_ref, sem, n_inner):
        self.w_hbm, self.layer = w_hbm, layer
        self.wb_refs, self.stage_ref, self.sem = wb_refs, stage_ref, sem
        k, self.tn = wb_refs[0].shape
        self.rows = k // n_inner
        self.n_inner = n_inner

    def _row_slice(self, chunk):
        start = chunk * self.rows
        if not isinstance(chunk, int):
            start = pl.multiple_of(start, self.rows)
        return pl.ds(start, self.rows)

    def _copy(self, tile, chunk, slot):
        src = self.w_hbm.at[self.layer, self._row_slice(chunk), pl.ds(tile * self.tn, self.tn)]
        return pltpu.make_async_copy(src, self.stage_ref.at[slot], self.sem.at[slot])

    def _land(self, tile, chunk, slot, dst_ref):
        self._copy(tile, chunk, slot).wait()
        dst_ref[self._row_slice(chunk), :] = self.stage_ref[slot].astype(BF16)

    def _load_first(self):
        self._copy(0, 0, 0).start()
        for chunk in range(self.n_inner):
            if chunk + 1 < self.n_inner:
                self._copy(0, chunk + 1, (chunk + 1) % 2).start()
            self._land(0, chunk, chunk % 2, self.wb_refs[0])

    def run(self, step):
        j, i = pl.program_id(0), pl.program_id(1)
        nxt = jnp.minimum(j + 1, pl.num_programs(0) - 1)

        @pl.when((j == 0) & (i == 0))
        def _():
            self._load_first()

        self._copy(nxt, i, 0).start(priority=1)
        for parity in range(2):
            @pl.when(j % 2 == parity)
            def _(parity=parity):
                step(self.wb_refs[parity])
        for parity in range(2):
            @pl.when(j % 2 == parity)
            def _(parity=parity):
                self._land(nxt, i, 0, self.wb_refs[1 - parity])


def _weight_tile_scratch(k, tn, n_inner):
    assert k % n_inner == 0 and (k // n_inner) % (2 * SUBLANES) == 0
    return [pltpu.VMEM((k, tn), BF16), pltpu.VMEM((k, tn), BF16),
            pltpu.VMEM((2, k // n_inner, tn), F32), pltpu.SemaphoreType.DMA((2,))]


def _mm_swiglu_kernel(a_ref, wg_ref, wu_ref, *rest, ada_layer, ada_kb):
    if ada_layer is None:
        o_ref, wgb_ref, wub_ref = rest
    else:
        c_ref, ada_hbm, o_ref, mod_ref, wgb_ref, wub_ref, stage_ref, sem = rest
        step = pl.program_id(0) * pl.num_programs(1) + pl.program_id(1)
        n_steps = pl.num_programs(0) * pl.num_programs(1)
        tk, tn9 = stage_ref.shape[1:]

        def chunk_copy(s):
            src = ada_hbm.at[ada_layer, pl.ds(pl.multiple_of((s % ada_kb) * tk, tk), tk),
                             pl.ds(pl.multiple_of((s // ada_kb) * tn9, LANES), tn9)]
            return pltpu.make_async_copy(src, stage_ref.at[s % 2], sem.at[s % 2])

        @pl.when(step == 0)
        def _():
            mod_ref[...] = jnp.zeros_like(mod_ref)
            chunk_copy(step).start(priority=1)

        chunk_copy(step).wait()

        @pl.when(step + 1 < n_steps)
        def _():
            chunk_copy(step + 1).start(priority=1)

    @pl.when(pl.program_id(1) == 0)
    def _():
        wgb_ref[...] = wg_ref[...].astype(BF16)
        wub_ref[...] = wu_ref[...].astype(BF16)
    a = a_ref[...]
    g = _dot(a, wgb_ref[...])
    u = _dot(a, wub_ref[...])
    o_ref[...] = (g * jax.nn.sigmoid(g) * u).astype(o_ref.dtype)
    if ada_layer is not None:
        c = c_ref[step % ada_kb]
        act = (c * jax.nn.sigmoid(c)).astype(BF16)
        mod_ref[step // ada_kb] += _dot(act, stage_ref[step % 2].astype(BF16))


def _mm_swiglu(a, w_in, layer, tm, tn, ada=None):
    m, k = a.shape
    f = w_in.shape[2] // 2
    nb = f // tn
    n_inner = m // tm
    in_specs = [pl.BlockSpec((tm, k), lambda j, i: (i, 0)),
                pl.BlockSpec((None, k, tn), lambda j, i: (layer, 0, j)),
                pl.BlockSpec((None, k, tn), lambda j, i: (layer, 0, j + nb))]
    out_specs = [pl.BlockSpec((tm, tn), lambda j, i: (i, j))]
    out_shape = [jax.ShapeDtypeStruct((m, f), BF16)]
    scratch = [pltpu.VMEM((k, tn), BF16), pltpu.VMEM((k, tn), BF16)]
    args = [a, w_in, w_in]
    ada_layer = ada_kb = None
    if ada is not None:
        c_pad, ada_w, ada_layer = ada
        rows, d = c_pad.shape
        n9 = ada_w.shape[2]
        steps = nb * n_inner
        ada_kb = d // ADA_SIDE_TK
        ada_nb = steps // ada_kb
        ok = (d % ADA_SIDE_TK == 0 and steps % ada_kb == 0 and ada_nb > 0 and n9 % ada_nb == 0
              and (n9 // ada_nb) % LANES == 0)
        if not ok:
            ada, ada_layer, ada_kb = None, None, None
    if ada is not None:
        tn9 = n9 // ada_nb
        c_blocks = c_pad.reshape(rows, ada_kb, ADA_SIDE_TK).transpose(1, 0, 2)
        in_specs += [pl.BlockSpec((ada_kb, rows, ADA_SIDE_TK), lambda j, i: (0, 0, 0)),
                     pl.BlockSpec(memory_space=pl.ANY)]
        out_specs.append(pl.BlockSpec((ada_nb, rows, tn9), lambda j, i: (0, 0, 0)))
        out_shape.append(jax.ShapeDtypeStruct((ada_nb, rows, tn9), F32))
        scratch += [pltpu.VMEM((2, ADA_SIDE_TK, tn9), F32), pltpu.SemaphoreType.DMA((2,))]
        args += [c_blocks, ada_w]
    outs = pl.pallas_call(
        functools.partial(_mm_swiglu_kernel, ada_layer=ada_layer, ada_kb=ada_kb),
        grid=(nb, n_inner),
        in_specs=in_specs,
        out_specs=out_specs,
        out_shape=out_shape,
        scratch_shapes=scratch,
        compiler_params=_params(2),
        name="mm_swiglu",
    )(*args)
    if ada is None:
        return outs[0], None
    mod = outs[1].transpose(1, 0, 2).reshape(rows, n9)
    return outs[0], mod


def _mm_resid_kernel(*refs, n_a, coef, layer, n_inner):
    a_refs = refs[:n_a]
    w_hbm, x_ref, g_ref, o_ref, wb0_ref, wb1_ref, stage_ref, sem = refs[n_a:]

    def step(wb_ref):
        acc = None
        k0 = 0
        for a_ref in a_refs:
            kw = a_ref.shape[1]
            part = _dot(a_ref[...], wb_ref[k0:k0 + kw, :])
            acc = part if acc is None else acc + part
            k0 += kw
        o_ref[...] = x_ref[...] + (coef * g_ref[...]) * acc

    _WeightTiles(w_hbm, layer, (wb0_ref, wb1_ref), stage_ref, sem, n_inner).run(step)


def _mm_resid(a_parts, w, layer, x, mod, row_of, j_gate, coef, seq, tm, tn):
    m = x.shape[0]
    k, n = w.shape[1], w.shape[2]
    per_seq = seq // tm
    n_inner = m // tm
    return pl.pallas_call(
        functools.partial(_mm_resid_kernel, n_a=len(a_parts), coef=coef, layer=layer,
                          n_inner=n_inner),
        grid=(n // tn, n_inner),
        in_specs=[pl.BlockSpec((tm, a.shape[1]), lambda j, i: (i, 0)) for a in a_parts] + [
                  pl.BlockSpec(memory_space=pl.ANY),
                  pl.BlockSpec((tm, tn), lambda j, i: (i, j)),
                  pl.BlockSpec((None, 1, tn), lambda j, i: (row_of(i // per_seq, j_gate), 0, j))],
        out_specs=pl.BlockSpec((tm, tn), lambda j, i: (i, j)),
        out_shape=jax.ShapeDtypeStruct((m, n), F32),
        scratch_shapes=_weight_tile_scratch(k, tn, n_inner),
        compiler_params=_params(2),
        name="mm_resid",
    )(*a_parts, w, x, mod)


def _shift_rows(x, k):
    assert k <= SUBLANES
    rolled = pltpu.roll(x, k, axis=0)
    row = lax.broadcasted_iota(jnp.int32, (SUBLANES, x.shape[1]), 0)
    top = jnp.where(row < k, 0.0, rolled[:SUBLANES])
    return jnp.concatenate([top, rolled[SUBLANES:]], axis=0)


def _pool_kernel(p_ref, w_ref, s_ref, o_ref):
    g = pl.program_id(0)
    for gi, win in enumerate(POOL_WINDOWS):
        @pl.when(g == gi)
        def _(win=win):
            x = p_ref[...]
            acc = x
            span = 1
            while span < win:
                acc = acc + _shift_rows(acc, span)
                span *= 2
            row = lax.broadcasted_iota(jnp.int32, (x.shape[0], 1), 0)
            count = jnp.minimum(row + 1, win).astype(F32)
            pooled = (acc / count - x).astype(BF16)
            o_ref[...] = (_dot(pooled, w_ref[...].astype(BF16)) * s_ref[...]).astype(o_ref.dtype)


def _pool(proj, pool_w, pool_scale, layer, batch, seq):
    n_groups, cg = pool_w.shape[1], pool_w.shape[2]
    assert n_groups == len(POOL_WINDOWS)
    m = proj.shape[0]
    return pl.pallas_call(
        _pool_kernel,
        grid=(n_groups, batch),
        in_specs=[pl.BlockSpec((seq, cg), lambda g, b: (b, g)),
                  pl.BlockSpec((None, None, cg, cg), lambda g, b: (layer, g, 0, 0)),
                  pl.BlockSpec((None, 1, cg), lambda g, b: (layer, 0, g))],
        out_specs=pl.BlockSpec((seq, cg), lambda g, b: (b, g)),
        out_shape=jax.ShapeDtypeStruct((m, n_groups * cg), BF16),
        compiler_params=_params(2),
        name="pool_mix",
    )(proj, pool_w, pool_scale)


def _prep_kernel(*refs, has_vres, lora_bounds, n_pairs, tiles_per_seq):
    if has_vres:
        (r_ref, k_ref, v_ref, lo_ref, rp_ref, kp_ref, vp_ref, lop_ref,
         mur_ref, muk_ref, muv_ref, mulo_ref, w0_ref, a0_ref, kk_ref, ka_ref, rk_ref,
         lw_ref, v0_ref, vf_ref,
         rt_ref, at_ref, kt_ref, bt_ref, kh_ref, bh_ref, vv_ref, gate_ref, bonus_ref, pc_ref,
         lwb_ref) = refs
    else:
        (r_ref, k_ref, v_ref, lo_ref, rp_ref, kp_ref, vp_ref, lop_ref,
         mur_ref, muk_ref, muv_ref, mulo_ref, w0_ref, a0_ref, kk_ref, ka_ref, rk_ref,
         lw_ref,
         rt_ref, at_ref, kt_ref, bt_ref, kh_ref, bh_ref, vv_ref, gate_ref, bonus_ref, pc_ref,
         lwb_ref) = refs

    @pl.when(pl.program_id(0) == 0)
    def _():
        lwb_ref[...] = lw_ref[...].astype(BF16)
    first = (pl.program_id(0) % tiles_per_seq) == 0
    tm = r_ref.shape[0]

    def shift_mix(cur_ref, prev_ref, mu_ref, cols):
        z = cur_ref[:, cols]
        rolled = pltpu.roll(z, 1, axis=0)
        row = lax.broadcasted_iota(jnp.int32, (SUBLANES, z.shape[1]), 0)
        prev_last = jnp.where(first, 0.0, prev_ref[SUBLANES - 1:SUBLANES, cols])
        top = jnp.where(row == 0, prev_last, rolled[:SUBLANES])
        zp = jnp.concatenate([top, rolled[SUBLANES:]], axis=0)
        return z + (zp - z) * mu_ref[:, cols]

    lo = shift_mix(lo_ref, lop_ref, mulo_ref, slice(None))
    e_w, e_a, e_g = lora_bounds
    lane = lax.broadcasted_iota(jnp.int32, lo.shape, 1)
    act = jnp.where(lane < e_w, jnp.tanh(lo),
                    jnp.where((lane >= e_a) & (lane < e_g), jax.nn.sigmoid(lo), lo)).astype(BF16)

    ri = lax.broadcasted_iota(jnp.int32, (tm, tm), 0)
    ci = lax.broadcasted_iota(jnp.int32, (tm, tm), 1)
    same = (ri // CHUNK) == (ci // CHUNK)
    sel = jnp.concatenate([(same & (ci <= ri)).astype(BF16),
                           (same & (ci > ri)).astype(BF16),
                           same.astype(BF16)], axis=0)
    ones = _head_ones(PAIR)

    lw_all = -DECAY_SCALE * jax.nn.sigmoid(w0_ref[...] + _dot(act, lwb_ref[0]))
    iclr_all = jax.nn.sigmoid(a0_ref[...] + _dot(act, lwb_ref[1]))
    gate_all = _dot(act, lwb_ref[2])
    if has_vres:
        mix_all = jax.nn.sigmoid(v0_ref[...] + _dot(act, lwb_ref[3]))

    for p in range(n_pairs):
        cols = slice(p * PAIR, (p + 1) * PAIR)
        r = shift_mix(r_ref, rp_ref, mur_ref, cols)
        k = shift_mix(k_ref, kp_ref, muk_ref, cols)
        v = shift_mix(v_ref, vp_ref, muv_ref, cols)
        lw = lw_all[:, cols]
        iclr = iclr_all[:, cols]
        gate_ref[p] = gate_all[:, cols]
        if has_vres:
            v = v + (vf_ref[p] - v) * mix_all[:, cols]
        vv_ref[p] = v

        kk = k * kk_ref[:, cols]
        kk = kk * lax.rsqrt(jnp.maximum(_head_sum(kk * kk, ones), L2_EPS * L2_EPS))
        km = k * (1 + (iclr - 1) * ka_ref[:, cols])
        bonus_ref[p] = _head_sum(r * km * rk_ref[:, cols], ones) * v
        a = -kk
        b = kk * iclr

        h3, m3, l3 = _split3(lw)
        sums = _dot(sel, h3) + _dot(sel, m3) + _dot(sel, l3)
        cum, rest, tot = sums[:tm], sums[tm:2 * tm], sums[2 * tm:]
        p_inv = jnp.exp(-cum)
        p_rest = jnp.exp(rest)
        rt_ref[p] = (r * jnp.exp(cum)).astype(BF16)
        at_ref[p] = (a * jnp.exp(cum - lw)).astype(BF16)
        kt_ref[p] = (km * p_inv).astype(BF16)
        bt_ref[p] = (b * p_inv).astype(BF16)
        kh_ref[p] = (km * p_rest).astype(BF16)
        bh_ref[p] = (b * p_rest).astype(BF16)
        p_tot = jnp.exp(tot)
        for c in range(tm // CHUNK):
            pc_ref[p, c] = p_tot[c * CHUNK:c * CHUNK + SUBLANES]


def _prep(proj, lora_in, vfirst, rows, mats, lora_bounds, batch, seq, d_rwkv, lo_blk, tm):
    m = proj.shape[0]
    has_vres = vfirst is not None
    n_pairs = d_rwkv // PAIR
    tiles_per_seq = seq // tm
    sub = tm // SUBLANES

    def cur(width, col):
        return pl.BlockSpec((tm, width), lambda i: (i, col))

    def prev(width, col):
        return pl.BlockSpec((SUBLANES, width), lambda i: (jnp.maximum(i * sub - 1, 0), col))

    def row(width):
        return pl.BlockSpec((1, width), lambda i: (0, 0))

    in_specs = [cur(d_rwkv, 1), cur(d_rwkv, 2), cur(d_rwkv, 3), cur(lo_blk, 0),
                prev(d_rwkv, 1), prev(d_rwkv, 2), prev(d_rwkv, 3), prev(lo_blk, 0),
                row(d_rwkv), row(d_rwkv), row(d_rwkv), row(lo_blk)]
    args = [proj] * 3 + [lora_in] + [proj] * 3 + [lora_in]
    args += [rows["mu_r"], rows["mu_k"], rows["mu_v"], rows["mu_lo"]]
    for name in ("w0", "a0", "k_k", "k_a", "r_k"):
        in_specs.append(row(d_rwkv))
        args.append(rows[name])
    in_specs.append(pl.BlockSpec(mats.shape, lambda i: (0, 0, 0), pipeline_mode=pl.Buffered(1)))
    args.append(mats)
    if has_vres:
        in_specs += [row(d_rwkv),
                     pl.BlockSpec((None, n_pairs, tm, PAIR),
                                  lambda i: (i // tiles_per_seq, 0, i % tiles_per_seq, 0))]
        args += [rows["v0"], vfirst]

    pair_spec = pl.BlockSpec((None, n_pairs, tm, PAIR),
                             lambda i: (i // tiles_per_seq, 0, i % tiles_per_seq, 0))
    def pair_shape(dtype):
        return jax.ShapeDtypeStruct((batch, n_pairs, seq, PAIR), dtype)
    pair_shapes = [pair_shape(BF16)] * 6 + [pair_shape(F32)] * 3
    cpt = tm // CHUNK
    pc_spec = pl.BlockSpec((None, n_pairs, cpt, SUBLANES, PAIR),
                           lambda i: (i // tiles_per_seq, 0, i % tiles_per_seq, 0, 0))
    pc_shape = jax.ShapeDtypeStruct((batch, n_pairs, seq // CHUNK, SUBLANES, PAIR), F32)
    return pl.pallas_call(
        functools.partial(_prep_kernel, has_vres=has_vres, lora_bounds=lora_bounds,
                          n_pairs=n_pairs, tiles_per_seq=tiles_per_seq),
        grid=(m // tm,),
        in_specs=in_specs,
        out_specs=[pair_spec] * 9 + [pc_spec],
        out_shape=pair_shapes + [pc_shape],
        scratch_shapes=[pltpu.VMEM(mats.shape, BF16)],
        compiler_params=_params(1),
        name="rwkv_prep",
    )(*args)


def _wkv_kernel(rt_ref, at_ref, kt_ref, bt_ref, kh_ref, bh_ref, v_ref, gate_ref, bonus_ref, pc_ref,
                lg_ref, lb_ref, o_ref, s_ref, y_ref):
    @pl.when(pl.program_id(2) == 0)
    def _():
        s_ref[...] = jnp.zeros_like(s_ref)

    pairs = rt_ref.shape[0]
    n_chunks = rt_ref.shape[1] // CHUNK
    head0 = lax.broadcasted_iota(jnp.int32, (CHUNK, PAIR), 1) < HEAD_SIZE
    row = lax.broadcasted_iota(jnp.int32, (CHUNK, 2 * PAIR), 0)
    col = lax.broadcasted_iota(jnp.int32, (CHUNK, 2 * PAIR), 1) % HEAD_SIZE
    strict = (row > col)[:, :PAIR]
    incl = row >= col
    eye = (row == col)[:, :PAIR].astype(F32)
    vi = lax.broadcasted_iota(jnp.int32, (PAIR, PAIR), 0) // HEAD_SIZE
    ki = lax.broadcasted_iota(jnp.int32, (PAIR, PAIR), 1) // HEAD_SIZE
    same_head = vi == ki

    def bd(x):
        x = x.astype(BF16)
        return jnp.concatenate([jnp.where(head0, x, 0.0), jnp.where(head0, 0.0, x)], axis=0)

    def mm(a, b_bd):
        return _dot(a.astype(BF16), b_bd)

    def chunk(c, carry):
        sl = pl.ds(pl.multiple_of(c * CHUNK, CHUNK), CHUNK)
        ps = range(pairs)
        ar = [jnp.concatenate([at_ref[p, sl, :], rt_ref[p, sl, :]], axis=0) for p in ps]
        bk = [jnp.concatenate([bd(bt_ref[p, sl, :]), bd(kt_ref[p, sl, :])], axis=0) for p in ps]
        v = [v_ref[p, sl, :].astype(BF16) for p in ps]
        v_bd = [bd(v[p]) for p in ps]
        s = [s_ref[p] for p in ps]
        amat = [_dot_nt(ar[p], bk[p]) for p in ps]
        xs = [_dot_nt(ar[p], s[p].astype(BF16)) for p in ps]
        a_ak = [jnp.where(strict, amat[p][:CHUNK, PAIR:], 0.0) for p in ps]
        rhs = [xs[p][:CHUNK] + mm(a_ak[p], v_bd[p]) for p in ps]

        pw = [jnp.where(strict, amat[p][:CHUNK, :PAIR], 0.0) for p in ps]
        inv = [eye + pw[p] for p in ps]
        pw = [mm(pw[p], bd(pw[p])) for p in ps]
        span = 2
        while 2 * span < CHUNK:
            both = [mm(jnp.concatenate([pw[p], inv[p]], axis=0), bd(pw[p])) for p in ps]
            inv = [inv[p] + both[p][CHUNK:] for p in ps]
            pw = [both[p][:CHUNK] for p in ps]
            span *= 2
        inv = [inv[p] + mm(inv[p], bd(pw[p])) for p in ps]

        u = [mm(inv[p], bd(rhs[p])).astype(BF16) for p in ps]
        a_r = [jnp.where(incl, amat[p][CHUNK:], 0.0) for p in ps]
        y = [xs[p][CHUNK:] + mm(a_r[p], jnp.concatenate([bd(u[p]), v_bd[p]], axis=0)) for p in ps]
        for p in ps:
            uv = jnp.concatenate([u[p], v[p]], axis=0)
            bkh = jnp.concatenate([bh_ref[p, sl, :], kh_ref[p, sl, :]], axis=0)
            upd = jnp.where(same_head, _dot_tn(uv, bkh), 0.0)
            y_ref[p, sl, :] = y[p]
            s_ref[p] = s[p] * pc_ref[p, c][0:1, :] + upd
        return carry

    lax.fori_loop(0, n_chunks, chunk, 0)

    averager = _head_ones(PAIR, 1.0 / HEAD_SIZE)
    for p in range(pairs):
        y = y_ref[p]
        mean = _head_sum(y, averager)
        yc = y - mean
        var = _head_sum(yc * yc, averager)
        yn = yc * lax.rsqrt(var + GN_EPS) * lg_ref[:, p * PAIR:(p + 1) * PAIR]
        yn = yn + lb_ref[:, p * PAIR:(p + 1) * PAIR] + bonus_ref[p]
        o_ref[:, p * PAIR:(p + 1) * PAIR] = (yn * gate_ref[p]).astype(o_ref.dtype)


def _wkv(prep_out, lnx_gain, lnx_bias, layer, batch, seq, tc, pairs_per_step):
    rt, at, kt, bt, kh, bh, vv, gate, bonus, pc = prep_out
    n_pairs = rt.shape[1]
    n_tc = seq // tc
    pb = pairs_per_step
    blk = pl.BlockSpec((None, pb, tc, PAIR), lambda b, p, c: (b, p, c, 0))
    pc_blk = pl.BlockSpec((None, pb, tc // CHUNK, SUBLANES, PAIR), lambda b, p, c: (b, p, c, 0, 0))
    row = pl.BlockSpec((None, 1, pb * PAIR), lambda b, p, c: (layer, 0, p))
    return pl.pallas_call(
        _wkv_kernel,
        grid=(batch, n_pairs // pb, n_tc),
        in_specs=[blk] * 9 + [pc_blk, row, row],
        out_specs=pl.BlockSpec((tc, pb * PAIR), lambda b, p, c: (b * n_tc + c, p)),
        out_shape=jax.ShapeDtypeStruct((batch * seq, n_pairs * PAIR), BF16),
        scratch_shapes=[pltpu.VMEM((pb, PAIR, PAIR), F32),
                        pltpu.VMEM((pb, tc, PAIR), F32)],
        compiler_params=_params(3),
        name="wkv7",
    )(rt, at, kt, bt, kh, bh, vv, gate, bonus, pc, lnx_gain, lnx_bias)


def _tile(n, target):
    t = min(n, target)
    while n % t:
        t //= 2
    return t


def kernel(x, c, ada_w, ada_b, norm_gain, ffn1_w_in, ffn1_w_out, ffn2_w_in, ffn2_w_out, mix_w_in,
           mix_w_in_vres, shift_mu, shift_mu_vres, pool_w, pool_scale, decay_w0, decay_w2, iclr_a0,
           iclr_a2, gate_g2, vres_v0, vres_v2, k_k, k_a, r_k, lnx_gain, lnx_bias, mix_w_out,
           final_gain):
    batch, seq, d = x.shape
    depth = ada_w.shape[0]
    m = batch * seq
    d_pool = pool_scale.shape[1]
    d_rwkv = decay_w0.shape[1]
    d_w, d_a, d_g = decay_w2.shape[1], iclr_a2.shape[1], gate_g2.shape[1]
    d_v = vres_v2.shape[1]
    lo_blk = d_w + d_a + d_g + d_v
    d_main = d_pool + 3 * d_rwkv
    assert d_pool == d_rwkv and lo_blk % LANES == 0
    assert d_rwkv % PAIR == 0 and seq % (2 * CHUNK) == 0
    lora_bounds = (d_w, d_w + d_a, d_w + d_a + d_g)

    tm_big = _tile(seq, 1024)
    tm_mid = _tile(seq, 512)
    tm_norm = _tile(seq, 512)
    tm_prep = 2 * CHUNK
    tc = _tile(seq, 256)
    pairs_per_step = _tile(d_rwkv // PAIR, 16)

    c_pad = jnp.pad(c, ((0, SUBLANES - batch), (0, 0)))
    ada_tiles = (_tile(d, 512), _tile(N_MOD * d, N_MOD * 1024))
    mod = _ada(c_pad, ada_w, ada_b, 0, *ada_tiles).reshape(SUBLANES * N_MOD, 1, d)
    next_mod = None
    gains = norm_gain.reshape(depth * 3, 1, d)
    pool_scale3 = pool_scale.reshape(depth, 1, d_pool)
    lnx_gain3 = lnx_gain.reshape(depth, 1, d_rwkv)
    lnx_bias3 = lnx_bias.reshape(depth, 1, d_rwkv)
    mix_w_in_t = jnp.swapaxes(mix_w_in, 1, 2)

    xf = x.reshape(m, d)
    vfirst = None
    for l in range(depth):
        def row_of(b, j):
            return b * N_MOD + j

        if l > 0:
            if next_mod is None:
                next_mod = _ada(c_pad, ada_w, ada_b, l, *ada_tiles)[0]
            else:
                next_mod = next_mod + ada_b[l]
            mod = next_mod.reshape(SUBLANES * N_MOD, 1, d)

        h = _norm_mod(xf, gains, 3 * l, mod, row_of, 0, 1, seq, tm_norm)
        act, next_mod = _mm_swiglu(h, ffn1_w_in, l, tm_big, _tile(ffn1_w_in.shape[2] // 2, 256),
                                   ada=(c_pad, ada_w, l + 1) if l + 1 < depth else None)
        xf = _mm_resid([act], ffn1_w_out, l, xf, mod, row_of, 2, 0.5, seq, tm_mid, _tile(d, 512))

        h = _norm_mod(xf, gains, 3 * l + 1, mod, row_of, 3, 4, seq, tm_norm)
        proj = _mm_plain(h, mix_w_in_t, l, d_main, tm_big, _tile(d_main, 512), F32)
        if l == 0:
            w_tail_t = jnp.zeros((d_v, d), F32)
            mu_tail = jnp.zeros((d_v,), F32)
        else:
            w_tail_t = mix_w_in_vres[l - 1].T
            mu_tail = shift_mu_vres[l - 1]
        w_lora_t = jnp.concatenate([mix_w_in_t[l, d_main:], w_tail_t], axis=0)[None]
        lora_in = _mm_plain(h, w_lora_t, 0, lo_blk, tm_big, lo_blk, F32)
        mu = jnp.concatenate([shift_mu[l], mu_tail])

        pool_out = _pool(proj, pool_w, pool_scale3, l, batch, seq)

        def pad_rows(w, start):
            return jnp.pad(w, ((start, lo_blk - start - w.shape[0]), (0, 0)))

        rows = {"mu_r": mu[0:d_rwkv], "mu_k": mu[d_rwkv:2 * d_rwkv], "mu_v": mu[2 * d_rwkv:3 * d_rwkv],
                "mu_lo": mu[3 * d_rwkv:], "w0": decay_w0[l], "a0": iclr_a0[l], "k_k": k_k[l],
                "k_a": k_a[l], "r_k": r_k[l].reshape(-1)}
        mats = [pad_rows(decay_w2[l], 0), pad_rows(iclr_a2[l], d_w), pad_rows(gate_g2[l], d_w + d_a)]
        if l > 0:
            rows["v0"] = vres_v0[l - 1]
            mats.append(pad_rows(vres_v2[l - 1], d_w + d_a + d_g))
        mats = jnp.stack(mats)
        rows = {k_: v_.reshape(1, -1) for k_, v_ in rows.items()}
        prep_out = _prep(proj, lora_in, vfirst, rows, mats, lora_bounds, batch, seq, d_rwkv, lo_blk,
                         tm_prep)
        if l == 0:
            vfirst = prep_out[6]
        rwkv_out = _wkv(prep_out, lnx_gain3, lnx_bias3, l, batch, seq, tc, pairs_per_step)

        xf = _mm_resid([pool_out, rwkv_out], mix_w_out, l, xf, mod, row_of, 5, 1.0, seq, tm_big,
                       _tile(d, 512))

        h = _norm_mod(xf, gains, 3 * l + 2, mod, row_of, 6, 7, seq, tm_norm)
        act, _ = _mm_swiglu(h, ffn2_w_in, l, tm_big, _tile(ffn2_w_in.shape[2] // 2, 256))
        xf = _mm_resid([act], ffn2_w_out, l, xf, mod, row_of, 8, 0.5, seq, tm_mid, _tile(d, 512))

    return _final_norm(xf, final_gain, tm_mid).reshape(batch, seq, d)
```
